```python
import math
import jax
import jax.numpy as jnp
from jax import lax
import numpy as np

D_MODEL = 2048
BATCH = 2
SEQ = 4096
DEPTH = 4
DEC_BATCH = 8
DEC_SEQ = 4096
PAST_LEN = 128

N_MIXERS = 3
N_GDN = (DEPTH + 2) // 3
N_SWA = (DEPTH + 1) // 3
N_FNET = DEPTH // 3

GDN_DK = 128
GDN_DV = 128
GDN_HK = D_MODEL // 128
GDN_HV = 2 * GDN_HK
GDN_QK_DIM = GDN_HK * GDN_DK
GDN_V_DIM = GDN_HV * GDN_DV
GDN_CONV_DIM = 2 * GDN_QK_DIM + GDN_V_DIM
GDN_IN_DIM = GDN_CONV_DIM + GDN_V_DIM + 4 * GDN_HV
GDN_CONV_W = 5
GDN_CHUNK = 64

SWA_DH = 128
SWA_HQ = D_MODEL // SWA_DH
SWA_HKV = 4
SWA_GROUP = SWA_HQ // SWA_HKV
SWA_IN_DIM = (SWA_HQ + 2 * SWA_HKV) * SWA_DH
WINDOW = 128
SWA_BLOCK = 128
REL_BUCKETS = 32
REL_MAX_DIST = 128

FNET_GROUPS = 4

FFN_HIDDEN = -(-8 * D_MODEL // (3 * 256)) * 256

RMS_EPS = 1e-6

kernel_name = 'hybrid_bidir_gdn_swa_fnet_encoder'


def _rmsnorm(x, g):
    xf = x.astype(jnp.float32)
    y = xf * lax.rsqrt(jnp.mean(xf * xf, axis=-1, keepdims=True) + RMS_EPS)
    return (y * g.astype(jnp.float32)).astype(x.dtype)


def _l2norm(x):
    xf = x.astype(jnp.float32)
    return xf * lax.rsqrt(jnp.sum(xf * xf, axis=-1, keepdims=True) + 1e-6)


def _centred_depthwise_conv(x, w):
    c = x.shape[-1]
    pad = GDN_CONV_W // 2
    return lax.conv_general_dilated(
        x, w[:, None, :].astype(x.dtype), window_strides=(1,), padding=[(pad, pad)],
        dimension_numbers=('NWC', 'WIO', 'NWC'), feature_group_count=c)


def _chunk_gated_delta_rule(q, k, v, g, beta):
    B, H, T, DK = k.shape
    DV = v.shape[-1]
    C = GDN_CHUNK
    n = T // C
    q, k, v = (t.reshape(B, H, n, C, t.shape[-1]) for t in (q, k, v))
    beta = beta.reshape(B, H, n, C, 1)
    g = jnp.cumsum(g.reshape(B, H, n, C), axis=-1)
    idx = jnp.arange(C)
    causal = idx[:, None] >= idx[None, :]
    strict = idx[:, None] > idx[None, :]
    decay = jnp.exp(jnp.where(causal, g[..., :, None] - g[..., None, :], -jnp.inf))
    k_beta = k * beta
    lower = jnp.where(strict, jnp.einsum('bhnik,bhnjk->bhnij', k_beta, k) * decay, 0.0)
    rhs = jnp.concatenate([v * beta, k_beta * jnp.exp(g)[..., None]], axis=-1)
    sol = lax.linalg.triangular_solve(lower + jnp.eye(C, dtype=jnp.float32), rhs,
                                      left_side=True, lower=True, unit_diagonal=True)
    u, w = sol[..., :DV], sol[..., DV:]
    qk = jnp.where(causal, jnp.einsum('bhnik,bhnjk->bhnij', q, k) * decay, 0.0)
    g_last = g[..., -1:]
    k_to_end = k * jnp.exp(g_last - g)[..., None]
    q_decay = q * jnp.exp(g)[..., None]
    xs = tuple(jnp.moveaxis(t, 2, 0) for t in (q_decay, k_to_end, u, w, qk, g_last))

    def step(S, inp):
        q_c, k_c, u_c, w_c, qk_c, gl_c = inp
        v_new = u_c - jnp.einsum('bhck,bhkv->bhcv', w_c, S)
        o_c = jnp.einsum('bhck,bhkv->bhcv', q_c, S) + jnp.einsum('bhij,bhjv->bhiv', qk_c, v_new)
        S = S * jnp.exp(gl_c)[..., None] + jnp.einsum('bhck,bhcv->bhkv', k_c, v_new)
        return S, o_c

    S0 = jnp.zeros((B, H, DK, DV), jnp.float32)
    _, o = lax.scan(step, S0, xs)
    return jnp.moveaxis(o, 0, 2).reshape(B, H, T, DV)


def _gated_deltanet(h, w_in, conv_w, a_log, dt_bias, norm_g, w_out):
    B, T, _ = h.shape
    proj = h @ w_in
    qkv, z, ba = jnp.split(proj, [GDN_CONV_DIM, GDN_CONV_DIM + GDN_V_DIM], axis=-1)
    qkv = jax.nn.silu(_centred_depthwise_conv(qkv, conv_w))
    q, k, v = jnp.split(qkv, [GDN_QK_DIM, 2 * GDN_QK_DIM], axis=-1)
    rep = GDN_HV // GDN_HK
    q = jnp.repeat(_l2norm(q.reshape(B, T, GDN_HK, GDN_DK)) * GDN_DK ** -0.5, rep, axis=2)
    k = jnp.repeat(_l2norm(k.reshape(B, T, GDN_HK, GDN_DK)), rep, axis=2)
    v = v.reshape(B, T, GDN_HV, GDN_DV).astype(jnp.float32)
    ba = ba.astype(jnp.float32).reshape(B, T, 2, 2, GDN_HV)
    beta = jax.nn.sigmoid(ba[:, :, :, 0])
    g = -jnp.exp(a_log.astype(jnp.float32)) * jax.nn.softplus(ba[:, :, :, 1] + dt_bias.astype(jnp.float32))
    q, k, v = (jnp.transpose(t, (0, 2, 1, 3)) for t in (q, k, v))
    beta = jnp.transpose(beta, (2, 0, 3, 1))
    g = jnp.transpose(g, (2, 0, 3, 1))
    o_fwd = _chunk_gated_delta_rule(q, k, v, g[0], beta[0])
    o_bwd = jnp.flip(_chunk_gated_delta_rule(jnp.flip(q, 2), jnp.flip(k, 2), jnp.flip(v, 2),
                                             jnp.flip(g[1], 2), jnp.flip(beta[1], 2)), 2)
    o = jnp.transpose(o_fwd + o_bwd, (0, 2, 1, 3))
    o = _rmsnorm(o, norm_g) * jax.nn.silu(z.reshape(B, T, GDN_HV, GDN_DV).astype(jnp.float32))
    return o.reshape(B, T, GDN_V_DIM).astype(h.dtype) @ w_out


def _band_rel_bias(rel_bias):
    qi = jnp.arange(SWA_BLOCK)[:, None]
    kj = jnp.arange(3 * SWA_BLOCK)[None, :]
    rel = kj - SWA_BLOCK - qi
    half = REL_BUCKETS // 2
    max_exact = half // 2
    n = jnp.abs(rel)
    large = max_exact + (jnp.log(jnp.maximum(n, 1).astype(jnp.float32) / max_exact)
                         / math.log(REL_MAX_DIST / max_exact) * (half - max_exact)).astype(jnp.int32)
    large = jnp.minimum(large, half - 1)
    bucket = jnp.where(rel > 0, half, 0) + jnp.where(n < max_exact, n, large)
    bias = rel_bias[bucket].astype(jnp.float32)
    bias = jnp.transpose(bias, (2, 0, 1)).reshape(SWA_HKV, SWA_GROUP, SWA_BLOCK, 3 * SWA_BLOCK)
    return bias, rel


def _swa_attention(h, w_in, sink, w_out, rel_bias):
    B, T, _ = h.shape
    nb = T // SWA_BLOCK
    qkv = h @ w_in
    q, k, v = jnp.split(qkv, [SWA_HQ * SWA_DH, (SWA_HQ + SWA_HKV) * SWA_DH], axis=-1)
    q = q.reshape(B, nb, SWA_BLOCK, SWA_HKV, SWA_GROUP, SWA_DH).transpose(1, 0, 2, 3, 4, 5)
    pad = ((0, 0), (SWA_BLOCK, SWA_BLOCK), (0, 0), (0, 0))
    k = jnp.pad(k.reshape(B, T, SWA_HKV, SWA_DH), pad)
    v = jnp.pad(v.reshape(B, T, SWA_HKV, SWA_DH), pad)
    bias, rel = _band_rel_bias(rel_bias)
    in_window = jnp.abs(rel) <= WINDOW
    sink_l = sink.astype(jnp.float32).reshape(1, SWA_HKV, SWA_GROUP, 1, 1)
    scale = SWA_DH ** -0.5
    kj = jnp.arange(3 * SWA_BLOCK)

    def block(args):
        qb, b = args
        start = b * SWA_BLOCK
        kb = lax.dynamic_slice_in_dim(k, start, 3 * SWA_BLOCK, axis=1)
        vb = lax.dynamic_slice_in_dim(v, start, 3 * SWA_BLOCK, axis=1)
        s = jnp.einsum('bqhgd,bkhd->bhgqk', qb, kb, preferred_element_type=jnp.float32) * scale + bias
        kpos = start - SWA_BLOCK + kj
        valid = in_window & ((kpos >= 0) & (kpos < T))[None, :]
        s = jnp.where(valid, s, -jnp.inf)
        m = jnp.maximum(jnp.max(s, axis=-1, keepdims=True), sink_l)
        p = jnp.exp(s - m)
        denom = jnp.sum(p, axis=-1, keepdims=True) + jnp.exp(sink_l - m)
        p = (p / denom).astype(vb.dtype)
        return jnp.einsum('bhgqk,bkhd->bqhgd', p, vb)

    o = lax.map(block, (q, jnp.arange(nb)))
    o = o.transpose(1, 0, 2, 3, 4, 5).reshape(B, T, SWA_HQ * SWA_DH)
    return o @ w_out


def _fourier_mix(h, w_out):
    B, T, D = h.shape
    hg = h.astype(jnp.float32).reshape(B, T, FNET_GROUPS, D // FNET_GROUPS)
    y = jnp.real(jnp.fft.fft2(hg, axes=(1, 3), norm='ortho'))
    return y.reshape(B, T, D).astype(h.dtype) @ w_out


def _swiglu(h, w_gate_up, w_down):
    gate, up = jnp.split(h @ w_gate_up, 2, axis=-1)
    return (jax.nn.silu(gate) * up) @ w_down


def _trunk(x, pre_mix_g, post_mix_g, pre_ffn_g, post_ffn_g, gdn_w_in, gdn_conv_w, gdn_a_log,
           gdn_dt_bias, gdn_norm_g, gdn_w_out, swa_w_in, swa_sink, swa_w_out, rel_bias,
           fnet_w_out, ffn_w_gate_up, ffn_w_down):
    for i in range(DEPTH):
        kind, j = i % N_MIXERS, i // N_MIXERS
        h = _rmsnorm(x, pre_mix_g[i])
        if kind == 0:
            m = _gated_deltanet(h, gdn_w_in[j], gdn_conv_w[j], gdn_a_log[j], gdn_dt_bias[j],
                                gdn_norm_g[j], gdn_w_out[j])
        elif kind == 1:
            m = _swa_attention(h, swa_w_in[j], swa_sink[j], swa_w_out[j], rel_bias)
        else:
            m = _fourier_mix(h, fnet_w_out[j])
        x = x + _rmsnorm(m, post_mix_g[i])
        h = _rmsnorm(x, pre_ffn_g[i])
        x = x + _rmsnorm(_swiglu(h, ffn_w_gate_up[i], ffn_w_down[i]), post_ffn_g[i])
    return x


def setup_inputs(seed: int = 0) -> dict:
    key = jax.random.key(seed)
    ks = jax.random.split(key, 20)
    f32 = jnp.float32

    def nrm(k, shape, scale):
        return jax.random.normal(k, shape, f32) * scale

    dt = jnp.exp(jax.random.uniform(ks[9], (N_GDN, 2, GDN_HV), f32, math.log(1e-3), math.log(1e-1)))
    return {
        'x_prompt': nrm(ks[0], (BATCH, SEQ, D_MODEL), 1.0),
        'x_sample': nrm(ks[1], (DEC_BATCH, DEC_SEQ, D_MODEL), 1.0),
        'pre_mix_g': 1.0 + nrm(ks[2], (DEPTH, D_MODEL), 0.02),
        'post_mix_g': 1.0 + nrm(ks[3], (DEPTH, D_MODEL), 0.02),
        'pre_ffn_g': 1.0 + nrm(ks[4], (DEPTH, D_MODEL), 0.02),
        'post_ffn_g': 1.0 + nrm(ks[5], (DEPTH, D_MODEL), 0.02),
        'gdn_w_in': nrm(ks[6], (N_GDN, D_MODEL, GDN_IN_DIM), D_MODEL ** -0.5),
        'gdn_conv_w': nrm(ks[7], (N_GDN, GDN_CONV_W, GDN_CONV_DIM), GDN_CONV_W ** -0.5),
        'gdn_a_log': jnp.log(jax.random.uniform(ks[8], (N_GDN, 2, GDN_HV), f32, 1.0, 16.0)),
        'gdn_dt_bias': dt + jnp.log(-jnp.expm1(-dt)),
        'gdn_norm_g': 1.0 + nrm(ks[10], (N_GDN, GDN_DV), 0.02),
        'gdn_w_out': nrm(ks[11], (N_GDN, GDN_V_DIM, D_MODEL), GDN_V_DIM ** -0.5),
        'swa_w_in': nrm(ks[12], (N_SWA, D_MODEL, SWA_IN_DIM), D_MODEL ** -0.5),
        'swa_sink': nrm(ks[13], (N_SWA, SWA_HQ), 0.5),
        'swa_w_out': nrm(ks[14], (N_SWA, SWA_HQ * SWA_DH, D_MODEL), (SWA_HQ * SWA_DH) ** -0.5),
        'rel_bias': nrm(ks[15], (REL_BUCKETS, SWA_HQ), 0.5),
        'fnet_w_out': nrm(ks[16], (N_FNET, D_MODEL, D_MODEL), D_MODEL ** -0.5),
        'ffn_w_gate_up': nrm(ks[17], (DEPTH, D_MODEL, 2 * FFN_HIDDEN), D_MODEL ** -0.5),
        'ffn_w_down': nrm(ks[18], (DEPTH, FFN_HIDDEN, D_MODEL), FFN_HIDDEN ** -0.5),
    }


def reference(x_prompt, x_sample, pre_mix_g, post_mix_g, pre_ffn_g, post_ffn_g, gdn_w_in,
              gdn_conv_w, gdn_a_log, gdn_dt_bias, gdn_norm_g, gdn_w_out, swa_w_in, swa_sink,
              swa_w_out, rel_bias, fnet_w_out, ffn_w_gate_up, ffn_w_down):
    y_prompt = _trunk(x_prompt, pre_mix_g, post_mix_g, pre_ffn_g, post_ffn_g, gdn_w_in, gdn_conv_w,
                      gdn_a_log, gdn_dt_bias, gdn_norm_g, gdn_w_out, swa_w_in, swa_sink, swa_w_out,
                      rel_bias, fnet_w_out, ffn_w_gate_up, ffn_w_down)
    y_sample = _trunk(x_sample, pre_mix_g, post_mix_g, pre_ffn_g, post_ffn_g, gdn_w_in, gdn_conv_w,
                      gdn_a_log, gdn_dt_bias, gdn_norm_g, gdn_w_out, swa_w_in, swa_sink, swa_w_out,
                      rel_bias, fnet_w_out, ffn_w_gate_up, ffn_w_down)
    return (y_prompt, y_sample)
```

```python
import functools
import math

import numpy as np
import jax
import jax.numpy as jnp
from jax import lax
from jax.experimental import pallas as pl
from jax.experimental.pallas import tpu as pltpu

D_MODEL = 2048
DEPTH = 4
N_MIXERS = 3

GDN_DK = 128
GDN_DV = 128
GDN_HK = D_MODEL // 128
GDN_HV = 2 * GDN_HK
GDN_QK_DIM = GDN_HK * GDN_DK
GDN_V_DIM = GDN_HV * GDN_DV
GDN_CONV_DIM = 2 * GDN_QK_DIM + GDN_V_DIM
GDN_CONV_W = 5
GDN_MAIN_DIM = GDN_CONV_DIM + GDN_V_DIM
GDN_GATE_DIM = 4 * GDN_HV
GDN_CHUNK = 128
GDN_GROUP = 4
GDN_INV_BASE = 16

SWA_DH = 128
SWA_HQ = D_MODEL // SWA_DH
SWA_HKV = 4
SWA_GROUP = SWA_HQ // SWA_HKV
WINDOW = 128
SWA_BLOCK = 128
REL_BUCKETS = 32
REL_MAX_DIST = 128

FNET_GROUPS = 4
FNET_GW = D_MODEL // FNET_GROUPS

FFN_HIDDEN = -(-8 * D_MODEL // (3 * 256)) * 256

RMS_EPS = 1e-6
NEG_BIG = -1e30

LANE = 128
BF16_SUBLANE = 16
VMEM_LIMIT_BYTES = 56 * 1024 * 1024

BF16 = jnp.bfloat16
F32 = jnp.float32


def _cparams(*sem):
    return pltpu.CompilerParams(dimension_semantics=sem, vmem_limit_bytes=VMEM_LIMIT_BYTES)


def _rms(x, g):
    return x * lax.rsqrt(jnp.mean(x * x, axis=-1, keepdims=True) + RMS_EPS) * g


def _silu(x):
    return x * (1.0 / (1.0 + jnp.exp(-x)))


def _dot(a, b):
    return jnp.dot(a, b, preferred_element_type=F32)


def _dot_nt(a, b):
    return lax.dot_general(a, b, (((1,), (1,)), ((), ())), preferred_element_type=F32)


def _norm_matmul_kernel(x_ref, g_ref, w_ref, o_ref, h_ref):
    @pl.when(pl.program_id(1) == 0)
    def _():
        h_ref[...] = _rms(x_ref[...], g_ref[...]).astype(h_ref.dtype)

    o_ref[...] = _dot(h_ref[...], w_ref[...]).astype(o_ref.dtype)


def _norm_matmul(x, g, w, out_dtype, tm, tn):
    n, d = x.shape
    m = w.shape[1]
    tm = min(tm, n)
    return pl.pallas_call(
        _norm_matmul_kernel,
        grid=(n // tm, m // tn),
        in_specs=[
            pl.BlockSpec((tm, d), lambda i, j: (i, 0)),
            pl.BlockSpec((1, d), lambda i, j: (0, 0)),
            pl.BlockSpec((d, tn), lambda i, j: (0, j)),
        ],
        out_specs=pl.BlockSpec((tm, tn), lambda i, j: (i, j)),
        out_shape=jax.ShapeDtypeStruct((n, m), out_dtype),
        scratch_shapes=[pltpu.VMEM((tm, d), BF16)],
        compiler_params=_cparams("parallel", "arbitrary"),
    )(x, g.reshape(1, d), w)


def _matmul_norm_res_kernel(*refs, gated):
    if gated:
        a_ref, z_ref, ng_ref, w_ref, x_ref, g_ref, o_ref, acc_ref = refs
    else:
        a_ref, w_ref, x_ref, g_ref, o_ref, acc_ref = refs
    k = pl.program_id(1)
    if gated:
        a = a_ref[...]
        z = z_ref[...].astype(F32)
        ng = ng_ref[...]
        parts = []
        for h in range(a.shape[1] // GDN_DV):
            sl = slice(h * GDN_DV, (h + 1) * GDN_DV)
            parts.append(_rms(a[:, sl], ng) * _silu(z[:, sl]))
        a = jnp.concatenate(parts, axis=1).astype(BF16)
    else:
        a = a_ref[...].astype(BF16)
    part = _dot(a, w_ref[...])

    @pl.when(k == 0)
    def _():
        acc_ref[...] = part

    @pl.when(k > 0)
    def _():
        acc_ref[...] += part

    @pl.when(k == pl.num_programs(1) - 1)
    def _():
        o_ref[...] = x_ref[...] + _rms(acc_ref[...], g_ref[...])


def _matmul_norm_res(a, w, x, g, tm, tk, gate=None):
    n, kdim = a.shape
    d = w.shape[1]
    in_specs = [pl.BlockSpec((tm, tk), lambda i, k: (i, k))]
    args = [a]
    if gate is not None:
        zsrc, zcol0, norm_g = gate
        zblk0 = zcol0 // tk
        in_specs += [pl.BlockSpec((tm, tk), lambda i, k: (i, zblk0 + k)),
                     pl.BlockSpec((1, GDN_DV), lambda i, k: (0, 0))]
        args += [zsrc, norm_g.reshape(1, GDN_DV)]
    in_specs += [pl.BlockSpec((tk, d), lambda i, k: (k, 0)),
                 pl.BlockSpec((tm, d), lambda i, k: (i, 0)),
                 pl.BlockSpec((1, d), lambda i, k: (0, 0))]
    args += [w, x, g.reshape(1, d)]
    return pl.pallas_call(
        functools.partial(_matmul_norm_res_kernel, gated=gate is not None),
        grid=(n // tm, kdim // tk),
        in_specs=in_specs,
        out_specs=pl.BlockSpec((tm, d), lambda i, k: (i, 0)),
        out_shape=jax.ShapeDtypeStruct((n, d), F32),
        scratch_shapes=[pltpu.VMEM((tm, d), F32)],
        compiler_params=_cparams("parallel", "arbitrary"),
    )(*args)


def _ffn_kernel(x_ref, g1_ref, wg_ref, wu_ref, wd_ref, g2_ref, o_ref, h_ref, acc_ref):
    f = pl.program_id(1)

    @pl.when(f == 0)
    def _():
        h_ref[...] = _rms(x_ref[...], g1_ref[...]).astype(BF16)

    h = h_ref[...]
    gate = _dot(h, wg_ref[...])
    up = _dot(h, wu_ref[...])
    part = _dot((_silu(gate) * up).astype(BF16), wd_ref[...])

    @pl.when(f == 0)
    def _():
        acc_ref[...] = part

    @pl.when(f > 0)
    def _():
        acc_ref[...] += part

    @pl.when(f == pl.num_programs(1) - 1)
    def _():
        o_ref[...] = x_ref[...] + _rms(acc_ref[...], g2_ref[...])


def _ffn(x, g1, w_gu, w_d, g2, tm, tf):
    n, d = x.shape
    fdim = w_d.shape[0]
    nf = fdim // tf
    return pl.pallas_call(
        _ffn_kernel,
        grid=(n // tm, nf),
        in_specs=[
            pl.BlockSpec((tm, d), lambda i, f: (i, 0)),
            pl.BlockSpec((1, d), lambda i, f: (0, 0)),
            pl.BlockSpec((d, tf), lambda i, f: (0, f)),
            pl.BlockSpec((d, tf), lambda i, f: (0, nf + f)),
            pl.BlockSpec((tf, d), lambda i, f: (f, 0)),
            pl.BlockSpec((1, d), lambda i, f: (0, 0)),
        ],
        out_specs=pl.BlockSpec((tm, d), lambda i, f: (i, 0)),
        out_shape=jax.ShapeDtypeStruct((n, d), F32),
        scratch_shapes=[pltpu.VMEM((tm, d), BF16), pltpu.VMEM((tm, d), F32)],
        compiler_params=_cparams("parallel", "arbitrary"),
    )(x, g1.reshape(1, d), w_gu, w_gu, w_d, g2.reshape(1, d))


def _gdn_gate_kernel(ba_ref, nega_ref, dtb_ref, o_ref, *, chunk):
    rows = ba_ref.shape[0]
    lane = lax.broadcasted_iota(jnp.int32, (chunk, GDN_GATE_DIM), 1)
    is_dt = (lane % (2 * GDN_HV)) >= GDN_HV
    is_bwd = lane >= 2 * GDN_HV
    ri = lax.broadcasted_iota(jnp.int32, (chunk, chunk), 0)
    ci = lax.broadcasted_iota(jnp.int32, (chunk, chunk), 1)
    tril = jnp.where(ri >= ci, 1.0, 0.0).astype(BF16)
    triu = jnp.where(ri <= ci, 1.0, 0.0).astype(BF16)
    for c in range(rows // chunk):
        x = ba_ref[c * chunk:(c + 1) * chunk, :]
        beta = 1.0 / (1.0 + jnp.exp(-x))
        y = x + dtb_ref[...]
        sp = jnp.maximum(y, 0.0) + jnp.log1p(jnp.exp(-jnp.abs(y)))
        g = jnp.where(is_dt, nega_ref[...] * sp, 0.0)
        hi = g.astype(BF16)
        lo = (g - hi.astype(F32)).astype(BF16)
        pre = _dot(tril, hi) + _dot(tril, lo)
        suf = _dot(triu, hi) + _dot(triu, lo)
        o_ref[c * chunk:(c + 1) * chunk, :] = jnp.where(is_dt, jnp.where(is_bwd, suf, pre), beta)


def _gdn_gates(ba, a_log, dt_bias, chunk, tt):
    n = ba.shape[0]
    zeros = jnp.zeros((2, GDN_HV), F32)
    nega = jnp.stack([zeros, -jnp.exp(a_log.astype(F32))], axis=1).reshape(1, GDN_GATE_DIM)
    dtb = jnp.stack([zeros, dt_bias.astype(F32)], axis=1).reshape(1, GDN_GATE_DIM)
    return pl.pallas_call(
        functools.partial(_gdn_gate_kernel, chunk=chunk),
        grid=(n // tt,),
        in_specs=[pl.BlockSpec((tt, GDN_GATE_DIM), lambda i: (i, 0)),
                  pl.BlockSpec((1, GDN_GATE_DIM), lambda i: (0, 0)),
                  pl.BlockSpec((1, GDN_GATE_DIM), lambda i: (0, 0))],
        out_specs=pl.BlockSpec((tt, GDN_GATE_DIM), lambda i: (i, 0)),
        out_shape=jax.ShapeDtypeStruct((n, GDN_GATE_DIM), F32),
        compiler_params=_cparams("parallel"),
    )(ba, nega, dtb)


def _gdn_prep_kernel(cur_ref, prev_ref, next_ref, cw_ref, o_ref, ext_ref):
    i = pl.program_id(1)
    j = pl.program_id(2)
    tt, tc = cur_ref.shape[1], cur_ref.shape[2]
    halo = prev_ref.shape[1]
    pad = GDN_CONV_W // 2
    ext_ref[0:halo, :] = jnp.where(i > 0, prev_ref[0].astype(F32), 0.0)
    ext_ref[halo:halo + tt, :] = cur_ref[0].astype(F32)
    ext_ref[halo + tt:2 * halo + tt, :] = jnp.where(i < pl.num_programs(1) - 1, next_ref[0].astype(F32), 0.0)
    acc = ext_ref[pl.ds(halo - pad, tt), :] * cw_ref[0:1, :]
    for w in range(1, GDN_CONV_W):
        acc = acc + ext_ref[pl.ds(halo - pad + w, tt), :] * cw_ref[w:w + 1, :]
    y = _silu(acc)
    col0 = j * tc
    is_q = col0 < GDN_QK_DIM
    is_qk = col0 < 2 * GDN_QK_DIM
    for h in range(tc // GDN_DK):
        sl = slice(h * GDN_DK, (h + 1) * GDN_DK)
        yh = y[:, sl]
        inv = lax.rsqrt(jnp.sum(yh * yh, axis=-1, keepdims=True) + 1e-6)
        f = jnp.where(is_q, inv * GDN_DK ** -0.5, jnp.where(is_qk, inv, 1.0))
        o_ref[0, :, sl] = (yh * f).astype(o_ref.dtype)


def _gdn_prep(proj, conv_w, tt, tc):
    b, t, _ = proj.shape
    halo = BF16_SUBLANE
    nh = t // halo
    r = tt // halo
    return pl.pallas_call(
        _gdn_prep_kernel,
        grid=(b, t // tt, GDN_CONV_DIM // tc),
        in_specs=[
            pl.BlockSpec((1, tt, tc), lambda bi, i, j: (bi, i, j)),
            pl.BlockSpec((1, halo, tc), lambda bi, i, j: (bi, jnp.maximum(i * r - 1, 0), j)),
            pl.BlockSpec((1, halo, tc), lambda bi, i, j: (bi, jnp.minimum((i + 1) * r, nh - 1), j)),
            pl.BlockSpec((GDN_CONV_W, tc), lambda bi, i, j: (0, j)),
        ],
        out_specs=pl.BlockSpec((1, tt, tc), lambda bi, i, j: (bi, i, j)),
        out_shape=jax.ShapeDtypeStruct((b, t, GDN_CONV_DIM), BF16),
        scratch_shapes=[pltpu.VMEM((tt + 2 * halo, tc), F32)],
        compiler_params=_cparams("parallel", "parallel", "parallel"),
    )(proj, proj, proj, conv_w)


def _tri_inverse_masks(c, upper):
    ri = lax.broadcasted_iota(jnp.int32, (c, c), 0)
    ci = lax.broadcasted_iota(jnp.int32, (c, c), 1)
    base_mask = (ri // GDN_INV_BASE) == (ci // GDN_INV_BASE)
    level_masks = []
    b = GDN_INV_BASE
    while b < c:
        mask = ((ri // (2 * b)) == (ci // (2 * b))) & ((ri // b) != (ci // b))
        level_masks.append(_take_blocks(mask, b, 0 if upper else 1))
        b *= 2
    return base_mask, level_masks


def _take_blocks(x, b, parity):
    return jnp.concatenate([x[s:s + b] for s in range(parity * b, x.shape[0], 2 * b)], axis=0)


def _unit_tri_inverse(low, eye, masks, upper):
    c = low.shape[0]
    base_mask, level_masks = masks
    m0 = jnp.where(base_mask, -low, 0.0)
    p = eye + m0
    mb = m0.astype(BF16)
    m = _dot(mb, mb)
    steps = int(math.log2(GDN_INV_BASE))
    for s in range(2, steps + 1):
        mb = m.astype(BF16)
        if s < steps:
            r = _dot(mb, jnp.concatenate([mb, p.astype(BF16)], axis=1))
            m = r[:, :c]
            p = p + r[:, c:]
        else:
            p = p + _dot(mb, p.astype(BF16))
    par = 0 if upper else 1
    b = GDN_INV_BASE
    for mask in level_masks:
        nblk = c // (2 * b)
        pb = p.astype(BF16)
        off = jnp.where(mask, _take_blocks(low, b, par), 0.0).astype(BF16)
        x = _dot(off, pb).astype(BF16)
        zero = jnp.zeros((b, c), BF16)
        pieces = []
        for i in range(nblk):
            xi = x[i * b:(i + 1) * b]
            pieces += [xi, zero] if upper else [zero, xi]
        y = _dot(_take_blocks(pb, b, par), jnp.concatenate(pieces, axis=0))
        rows = []
        for i in range(nblk):
            keep = p[(2 * i + 1 - par) * b:(2 * i + 2 - par) * b]
            upd = p[(2 * i + par) * b:(2 * i + par + 1) * b] - y[i * b:(i + 1) * b]
            rows += [upd, keep] if upper else [keep, upd]
        p = jnp.concatenate(rows, axis=0)
        b *= 2
    return p


def _gdn_core_kernel(q_ref, k_ref, v_ref, gc_ref, gr_ref, o_ref, s_ref, *, group, chunk):
    t = q_ref.shape[1]
    nc = t // chunk
    ri = lax.broadcasted_iota(jnp.int32, (chunk, chunk), 0)
    ci = lax.broadcasted_iota(jnp.int32, (chunk, chunk), 1)
    eye = jnp.where(ri == ci, 1.0, 0.0).astype(F32)
    incl = (ri >= ci, ri <= ci)
    strict = (ri > ci, ri < ci)
    inv_masks = (_tri_inverse_masks(chunk, False), _tri_inverse_masks(chunk, True))
    s_ref[...] = jnp.zeros(s_ref.shape, F32)
    o_ref[...] = jnp.zeros(o_ref.shape, F32)

    def body(c, carry):
        for d in range(2):
            cc = c if d == 0 else nc - 1 - c
            r0 = pl.multiple_of(cc * chunk, chunk)
            gcol = gc_ref[0, 0, pl.ds(r0, chunk), :]
            grow = gr_ref[0, 0, :, pl.ds(r0, chunk)]
            edge = chunk - 1 if d == 0 else 0
            for kh in range(group // 2):
                ksl = slice(kh * GDN_DK, (kh + 1) * GDN_DK)
                qb = q_ref[0, pl.ds(r0, chunk), ksl]
                kb = k_ref[0, pl.ds(r0, chunk), ksl]
                qf = qb.astype(F32)
                kf = kb.astype(F32)
                gram = _dot_nt(jnp.concatenate([qb, kb], axis=0), kb)
                qk, kk = gram[:chunk], gram[chunk:]
                kt = kf.T
                for jj in range(2):
                    j = 2 * kh + jj
                    cb = d * 2 * group + j
                    cg = cb + group
                    beta = gcol[:, cb:cb + 1]
                    gcum = gcol[:, cg:cg + 1]
                    gcum_r = grow[cg:cg + 1, :]
                    glast = gcol[edge:edge + 1, cg:cg + 1]
                    dec = jnp.where(incl[d], jnp.exp(gcum - gcum_r), 0.0)
                    low = jnp.where(strict[d], kk * dec, 0.0) * beta
                    tinv = _unit_tri_inverse(low, eye, inv_masks[d], upper=(d == 1))
                    eg = jnp.exp(gcum)
                    vsl = slice(j * GDN_DV, (j + 1) * GDN_DV)
                    vf = v_ref[0, pl.ds(r0, chunk), vsl].astype(F32)
                    kbeta = kf * beta
                    rhs = jnp.concatenate([vf * beta, kbeta * eg], axis=1).astype(BF16)
                    sol = _dot(tinv.astype(BF16), rhs)
                    u, w = sol[:, :GDN_DV], sol[:, GDN_DV:]
                    si = d * group + j
                    s = s_ref[si]
                    ws_qs = _dot(jnp.concatenate([w, qf * eg], axis=0).astype(BF16), s.astype(BF16))
                    v_new = (u - ws_qs[:chunk]).astype(BF16)
                    kend_t = kt * jnp.exp(glast - gcum_r)
                    lhs = jnp.concatenate([(qk * dec).astype(BF16), kend_t.astype(BF16)], axis=0)
                    r = _dot(lhs, v_new)
                    o_ref[0, pl.ds(r0, chunk), vsl] += ws_qs[chunk:] + r[:chunk]
                    s_ref[si] = s * jnp.exp(glast) + r[chunk:]
        return carry

    lax.fori_loop(0, nc, body, 0)


def _gdn_core(qkv, gcol, grow, group, chunk):
    b, t, _ = qkv.shape
    ng = GDN_HV // group
    qw = (group // 2) * GDN_DK
    vw = group * GDN_DV
    kblk0 = GDN_QK_DIM // qw
    vblk0 = 2 * GDN_QK_DIM // vw
    return pl.pallas_call(
        functools.partial(_gdn_core_kernel, group=group, chunk=chunk),
        grid=(b, ng),
        in_specs=[
            pl.BlockSpec((1, t, qw), lambda bi, hg: (bi, 0, hg)),
            pl.BlockSpec((1, t, qw), lambda bi, hg: (bi, 0, kblk0 + hg)),
            pl.BlockSpec((1, t, vw), lambda bi, hg: (bi, 0, vblk0 + hg)),
            pl.BlockSpec((1, 1, t, 4 * group), lambda bi, hg: (bi, hg, 0, 0)),
            pl.BlockSpec((1, 1, 4 * group, t), lambda bi, hg: (bi, hg, 0, 0)),
        ],
        out_specs=pl.BlockSpec((1, t, vw), lambda bi, hg: (bi, 0, hg)),
        out_shape=jax.ShapeDtypeStruct((b, t, GDN_V_DIM), F32),
        scratch_shapes=[pltpu.VMEM((2 * group, GDN_DK, GDN_DV), F32)],
        compiler_params=_cparams("parallel", "parallel"),
    )(qkv, qkv, qkv, gcol, grow)


def _gated_deltanet_layer(x, p, j, li):
    b, t, d = x.shape
    n = b * t
    x2 = x.reshape(n, d)
    g_pre = p['pre_mix_g'][li]
    proj = _norm_matmul(x2, g_pre, p['gdn_w_main'][j], BF16, 1024, 1024)
    ba = _norm_matmul(x2, g_pre, p['gdn_w_gate'][j], F32, 1024, GDN_GATE_DIM)
    gates = _gdn_gates(ba, p['gdn_a_log'][j], p['gdn_dt_bias'][j], GDN_CHUNK, 512)
    grp = GDN_GROUP
    ng = GDN_HV // grp
    gcol = gates.reshape(b, t, 4, ng, grp).transpose(0, 3, 1, 2, 4).reshape(b, ng, t, 4 * grp)
    grow = jnp.swapaxes(gcol, 2, 3)
    qkv = _gdn_prep(proj.reshape(b, t, GDN_MAIN_DIM), p['gdn_conv_w'][j], 512, 512)
    o = _gdn_core(qkv, gcol, grow, grp, GDN_CHUNK)
    y = _matmul_norm_res(o.reshape(n, GDN_V_DIM), p['gdn_w_out'][j], x2, p['post_mix_g'][li], 512, 512,
                         gate=(proj, GDN_CONV_DIM, p['gdn_norm_g'][j]))
    return y.reshape(b, t, d)


def _rel_bucket_table():
    qi = np.arange(SWA_BLOCK)[:, None]
    kj = np.arange(3 * SWA_BLOCK)[None, :]
    rel = kj - SWA_BLOCK - qi
    half = REL_BUCKETS // 2
    max_exact = half // 2
    nabs = np.abs(rel)
    large = max_exact + (np.log(np.maximum(nabs, 1).astype(np.float32) / max_exact)
                         / math.log(REL_MAX_DIST / max_exact) * (half - max_exact)).astype(np.int32)
    large = np.minimum(large, half - 1)
    bucket = np.where(rel > 0, half, 0) + np.where(nabs < max_exact, nabs, large)
    return np.where(nabs <= WINDOW, bucket, -1).astype(np.int32)


def _swa_kernel(q_ref, kp_ref, kc_ref, kn_ref, vp_ref, vc_ref, vn_ref, bucket_ref, relb_ref, sink_ref,
                o_ref, bias_ref):
    b, i, h = pl.program_id(0), pl.program_id(1), pl.program_id(2)
    nb = pl.num_programs(1)
    blk = SWA_BLOCK

    @pl.when((b == 0) & (i == 0) & (h == 0))
    def _():
        bucket = bucket_ref[...]
        for hq in range(SWA_HQ):
            acc = jnp.full(bucket.shape, NEG_BIG, F32)
            for bk in range(REL_BUCKETS):
                acc = jnp.where(bucket == bk, relb_ref[bk, hq], acc)
            bias_ref[hq * blk:(hq + 1) * blk, :] = acc

    rows = SWA_GROUP * blk
    q = q_ref[0]
    qs = jnp.concatenate([q[:, g * SWA_DH:(g + 1) * SWA_DH] for g in range(SWA_GROUP)], axis=0)
    kcat = jnp.concatenate([kp_ref[0], kc_ref[0], kn_ref[0]], axis=0)
    vcat = jnp.concatenate([vp_ref[0], vc_ref[0], vn_ref[0]], axis=0)
    s = _dot_nt(qs, kcat) * SWA_DH ** -0.5 + bias_ref[pl.ds(pl.multiple_of(h * rows, rows), rows), :]
    col = lax.broadcasted_iota(jnp.int32, s.shape, 1)
    outside = ((col < blk) & (i == 0)) | ((col >= 2 * blk) & (i == nb - 1))
    s = jnp.where(outside, NEG_BIG, s)
    sink = jnp.concatenate([jnp.full((blk, 1), sink_ref[h * SWA_GROUP + g], F32) for g in range(SWA_GROUP)], axis=0)
    m = jnp.maximum(jnp.max(s, axis=-1, keepdims=True), sink)
    pexp = jnp.exp(s - m)
    denom = jnp.sum(pexp, axis=-1, keepdims=True) + jnp.exp(sink - m)
    o = _dot(pexp.astype(BF16), vcat) / denom
    for g in range(SWA_GROUP):
        o_ref[0, :, g * SWA_DH:(g + 1) * SWA_DH] = o[g * blk:(g + 1) * blk].astype(o_ref.dtype)


def _swa_attention(qkv, rel_bias, sink):
    b, t, _ = qkv.shape
    nb = t // SWA_BLOCK
    gw = SWA_GROUP * SWA_DH
    k0 = SWA_HQ
    v0 = SWA_HQ + SWA_HKV
    kv_spec = lambda off, blk0: pl.BlockSpec(
        (1, SWA_BLOCK, SWA_DH), lambda bi, i, h: (bi, jnp.clip(i + off, 0, nb - 1), blk0 + h))
    return pl.pallas_call(
        _swa_kernel,
        grid=(b, nb, SWA_HKV),
        in_specs=[
            pl.BlockSpec((1, SWA_BLOCK, gw), lambda bi, i, h: (bi, i, h)),
            kv_spec(-1, k0), kv_spec(0, k0), kv_spec(1, k0),
            kv_spec(-1, v0), kv_spec(0, v0), kv_spec(1, v0),
            pl.BlockSpec((SWA_BLOCK, 3 * SWA_BLOCK), lambda bi, i, h: (0, 0)),
            pl.BlockSpec(memory_space=pltpu.SMEM),
            pl.BlockSpec(memory_space=pltpu.SMEM),
        ],
        out_specs=pl.BlockSpec((1, SWA_BLOCK, gw), lambda bi, i, h: (bi, i, h)),
        out_shape=jax.ShapeDtypeStruct((b, t, SWA_HQ * SWA_DH), BF16),
        scratch_shapes=[pltpu.VMEM((SWA_HQ * SWA_BLOCK, 3 * SWA_BLOCK), F32)],
        compiler_params=_cparams("arbitrary", "arbitrary", "arbitrary"),
    )(qkv, qkv, qkv, qkv, qkv, qkv, qkv, jnp.asarray(_rel_bucket_table()), rel_bias.astype(F32),
      sink.astype(F32))


def _swa_layer(x, p, j, li):
    b, t, d = x.shape
    n = b * t
    x2 = x.reshape(n, d)
    qkv = _norm_matmul(x2, p['pre_mix_g'][li], p['swa_w_in'][j], BF16, 1024, 1024)
    o = _swa_attention(qkv.reshape(b, t, -1), p['rel_bias'], p['swa_sink'][j])
    y = _matmul_norm_res(o.reshape(n, d), p['swa_w_out'][j], x2, p['post_mix_g'][li], 512, 512)
    return y.reshape(b, t, d)


def _fnet_chan_kernel(x_ref, g_ref, w_ref, o_ref, h_ref):
    gi = pl.program_id(1)

    @pl.when(gi == 0)
    def _():
        h_ref[...] = _rms(x_ref[...], g_ref[...]).astype(BF16)

    hg = h_ref[:, pl.ds(pl.multiple_of(gi * FNET_GW, FNET_GW), FNET_GW)]
    r = _dot(hg, w_ref[...])
    o_ref[0, 0] = r[:, :FNET_GW].astype(o_ref.dtype)
    o_ref[0, 1] = r[:, FNET_GW:].astype(o_ref.dtype)


def _fnet_chan(x, g, w_cs, tm):
    b, t, d = x.shape
    tm = min(tm, t)
    nt = t // tm
    return pl.pallas_call(
        _fnet_chan_kernel,
        grid=(b * nt, FNET_GROUPS),
        in_specs=[
            pl.BlockSpec((tm, d), lambda i, gi: (i, 0)),
            pl.BlockSpec((1, d), lambda i, gi: (0, 0)),
            pl.BlockSpec((FNET_GW, 2 * FNET_GW), lambda i, gi: (0, 0)),
        ],
        out_specs=pl.BlockSpec((1, 2, tm, FNET_GW), lambda i, gi: (i // nt, 0, i % nt, gi)),
        out_shape=jax.ShapeDtypeStruct((b, 2, t, d), BF16),
        scratch_shapes=[pltpu.VMEM((tm, d), BF16)],
        compiler_params=_cparams("parallel", "arbitrary"),
    )(x.reshape(b * t, d), g.reshape(1, d), w_cs)


def _bmm_kernel(a_ref, b_ref, o_ref, acc_ref):
    k = pl.program_id(2)
    part = _dot(a_ref[...], b_ref[0])

    @pl.when(k == 0)
    def _():
        acc_ref[...] = part

    @pl.when(k > 0)
    def _():
        acc_ref[...] += part

    @pl.when(k == pl.num_programs(2) - 1)
    def _():
        o_ref[0] = acc_ref[...].astype(o_ref.dtype)


def _shared_lhs_bmm(a, bm, tm, tk):
    m, kdim = a.shape
    b, _, d = bm.shape
    tm = min(tm, m)
    return pl.pallas_call(
        _bmm_kernel,
        grid=(b, m // tm, kdim // tk),
        in_specs=[
            pl.BlockSpec((tm, tk), lambda bi, i, k: (i, k)),
            pl.BlockSpec((1, tk, d), lambda bi, i, k: (bi, k, 0)),
        ],
        out_specs=pl.BlockSpec((1, tm, d), lambda bi, i, k: (bi, i, 0)),
        out_shape=jax.ShapeDtypeStruct((b, m, d), BF16),
        scratch_shapes=[pltpu.VMEM((tm, d), F32)],
        compiler_params=_cparams("parallel", "parallel", "arbitrary"),
    )(a, bm)


def _dft_tables(t):
    def angle(n):
        idx = jnp.arange(n, dtype=jnp.int32)
        return (idx[:, None] * idx[None, :] % n).astype(F32) * (2.0 * math.pi / n)

    ac = angle(FNET_GW)
    at = angle(t)
    scale = 1.0 / math.sqrt(t * FNET_GW)
    w_cs = jnp.concatenate([jnp.cos(ac), jnp.sin(ac)], axis=1).astype(BF16)
    f_t = (jnp.concatenate([jnp.cos(at), -jnp.sin(at)], axis=1) * scale).astype(BF16)
    return w_cs, f_t


def _fnet_layer(x, p, j, li):
    b, t, d = x.shape
    n = b * t
    w_cs, f_t = _dft_tables(t)
    pcs = _fnet_chan(x, p['pre_mix_g'][li], w_cs, 1024)
    y = _shared_lhs_bmm(f_t, pcs.reshape(b, 2 * t, d), 1024, 1024)
    out = _matmul_norm_res(y.reshape(n, d), p['fnet_w_out'][j], x.reshape(n, d), p['post_mix_g'][li], 512, 512)
    return out.reshape(b, t, d)


def _trunk(x, p):
    b, t, d = x.shape
    for i in range(DEPTH):
        kind, j = i % N_MIXERS, i // N_MIXERS
        if kind == 0:
            x = _gated_deltanet_layer(x, p, j, i)
        elif kind == 1:
            x = _swa_layer(x, p, j, i)
        else:
            x = _fnet_layer(x, p, j, i)
        x = _ffn(x.reshape(b * t, d), p['pre_ffn_g'][i], p['ffn_w_gate_up'][i], p['ffn_w_down'][i],
                 p['post_ffn_g'][i], 512, 512).reshape(b, t, d)
    return x


def kernel(x_prompt, x_sample, pre_mix_g, post_mix_g, pre_ffn_g, post_ffn_g, gdn_w_in, gdn_conv_w, gdn_a_log,
           gdn_dt_bias, gdn_norm_g, gdn_w_out, swa_w_in, swa_sink, swa_w_out, rel_bias, fnet_w_out,
           ffn_w_gate_up, ffn_w_down):
    p = dict(
        pre_mix_g=pre_mix_g, post_mix_g=post_mix_g, pre_ffn_g=pre_ffn_g, post_ffn_g=post_ffn_g,
        gdn_w_main=gdn_w_in[:, :, :GDN_MAIN_DIM].astype(BF16),
        gdn_w_gate=gdn_w_in[:, :, GDN_MAIN_DIM:].astype(BF16),
        gdn_conv_w=gdn_conv_w, gdn_a_log=gdn_a_log, gdn_dt_bias=gdn_dt_bias, gdn_norm_g=gdn_norm_g,
        gdn_w_out=gdn_w_out.astype(BF16),
        swa_w_in=swa_w_in.astype(BF16), swa_sink=swa_sink, swa_w_out=swa_w_out.astype(BF16),
        rel_bias=rel_bias, fnet_w_out=fnet_w_out.astype(BF16),
        ffn_w_gate_up=ffn_w_gate_up.astype(BF16), ffn_w_down=ffn_w_down.astype(BF16),
    )
    return _trunk(x_prompt, p), _trunk(x_sample, p)
```

```python
import functools
import math

import numpy as np
import jax
import jax.numpy as jnp
from jax import lax
from jax.experimental import pallas as pl
from jax.experimental.pallas import tpu as pltpu

D_MODEL = 2048
DEPTH = 4
N_MIXERS = 3

GDN_DK = 128
GDN_DV = 128
GDN_HK = D_MODEL // 128
GDN_HV = 2 * GDN_HK
GDN_QK_DIM = GDN_HK * GDN_DK
GDN_V_DIM = GDN_HV * GDN_DV
GDN_CONV_DIM = 2 * GDN_QK_DIM + GDN_V_DIM
GDN_CONV_W = 5
GDN_MAIN_DIM = GDN_CONV_DIM + GDN_V_DIM
GDN_GATE_DIM = 4 * GDN_HV
GDN_CHUNK = 128
GDN_GROUP = 4
GDN_INV_BASE = 16

SWA_DH = 128
SWA_HQ = D_MODEL // SWA_DH
SWA_HKV = 4
SWA_GROUP = SWA_HQ // SWA_HKV
WINDOW = 128
SWA_BLOCK = 128
REL_BUCKETS = 32
REL_MAX_DIST = 128

FNET_GROUPS = 4
FNET_GW = D_MODEL // FNET_GROUPS

FFN_HIDDEN = -(-8 * D_MODEL // (3 * 256)) * 256

RMS_EPS = 1e-6
NEG_BIG = -1e30

LANE = 128
BF16_SUBLANE = 16
VMEM_LIMIT_BYTES = 56 * 1024 * 1024

BF16 = jnp.bfloat16
F32 = jnp.float32


def _cparams(*sem):
    return pltpu.CompilerParams(dimension_semantics=sem, vmem_limit_bytes=VMEM_LIMIT_BYTES)


def _rms(x, g):
    return x * lax.rsqrt(jnp.mean(x * x, axis=-1, keepdims=True) + RMS_EPS) * g


def _silu(x):
    return x * (1.0 / (1.0 + jnp.exp(-x)))


def _dot(a, b):
    return jnp.dot(a, b, preferred_element_type=F32)


def _dot_nt(a, b):
    return lax.dot_general(a, b, (((1,), (1,)), ((), ())), preferred_element_type=F32)


def _norm_matmul_kernel(x_ref, g_ref, w_ref, o_ref, h_ref):
    @pl.when(pl.program_id(1) == 0)
    def _():
        h_ref[...] = _rms(x_ref[...], g_ref[...]).astype(h_ref.dtype)

    o_ref[...] = _dot(h_ref[...], w_ref[...]).astype(o_ref.dtype)


def _norm_matmul(x, g, w, out_dtype, tm, tn):
    n, d = x.shape
    m = w.shape[1]
    tm = min(tm, n)
    return pl.pallas_call(
        _norm_matmul_kernel,
        grid=(n // tm, m // tn),
        in_specs=[
            pl.BlockSpec((tm, d), lambda i, j: (i, 0)),
            pl.BlockSpec((1, d), lambda i, j: (0, 0)),
            pl.BlockSpec((d, tn), lambda i, j: (0, j)),
        ],
        out_specs=pl.BlockSpec((tm, tn), lambda i, j: (i, j)),
        out_shape=jax.ShapeDtypeStruct((n, m), out_dtype),
        scratch_shapes=[pltpu.VMEM((tm, d), BF16)],
        compiler_params=_cparams("parallel", "arbitrary"),
    )(x, g.reshape(1, d), w)


def _matmul_norm_res_kernel(*refs, gated):
    if gated:
        a_ref, z_ref, ng_ref, w_ref, x_ref, g_ref, o_ref, acc_ref = refs
    else:
        a_ref, w_ref, x_ref, g_ref, o_ref, acc_ref = refs
    k = pl.program_id(1)
    if gated:
        a = a_ref[...]
        z = z_ref[...].astype(F32)
        ng = ng_ref[...]
        parts = []
        for h in range(a.shape[1] // GDN_DV):
            sl = slice(h * GDN_DV, (h + 1) * GDN_DV)
            parts.append(_rms(a[:, sl], ng) * _silu(z[:, sl]))
        a = jnp.concatenate(parts, axis=1).astype(BF16)
    else:
        a = a_ref[...].astype(BF16)
    part = _dot(a, w_ref[...])

    @pl.when(k == 0)
    def _():
        acc_ref[...] = part

    @pl.when(k > 0)
    def _():
        acc_ref[...] += part

    @pl.when(k == pl.num_programs(1) - 1)
    def _():
        o_ref[...] = x_ref[...] + _rms(acc_ref[...], g_ref[...])


def _matmul_norm_res(a, w, x, g, tm, tk, gate=None):
    n, kdim = a.shape
    d = w.shape[1]
    in_specs = [pl.BlockSpec((tm, tk), lambda i, k: (i, k))]
    args = [a]
    if gate is not None:
        zsrc, zcol0, norm_g = gate
        zblk0 = zcol0 // tk
        in_specs += [pl.BlockSpec((tm, tk), lambda i, k: (i, zblk0 + k)),
                     pl.BlockSpec((1, GDN_DV), lambda i, k: (0, 0))]
        args += [zsrc, norm_g.reshape(1, GDN_DV)]
    in_specs += [pl.BlockSpec((tk, d), lambda i, k: (k, 0)),
                 pl.BlockSpec((tm, d), lambda i, k: (i, 0)),
                 pl.BlockSpec((1, d), lambda i, k: (0, 0))]
    args += [w, x, g.reshape(1, d)]
    return pl.pallas_call(
        functools.partial(_matmul_norm_res_kernel, gated=gate is not None),
        grid=(n // tm, kdim // tk),
        in_specs=in_specs,
        out_specs=pl.BlockSpec((tm, d), lambda i, k: (i, 0)),
        out_shape=jax.ShapeDtypeStruct((n, d), F32),
        scratch_shapes=[pltpu.VMEM((tm, d), F32)],
        compiler_params=_cparams("parallel", "arbitrary"),
    )(*args)


def _ffn_kernel(x_ref, g1_ref, wg_ref, wu_ref, wd_ref, g2_ref, o_ref, h_ref, acc_ref):
    f = pl.program_id(1)

    @pl.when(f == 0)
    def _():
        h_ref[...] = _rms(x_ref[...], g1_ref[...]).astype(BF16)

    h = h_ref[...]
    gate = _dot(h, wg_ref[...])
    up = _dot(h, wu_ref[...])
    part = _dot((_silu(gate) * up).astype(BF16), wd_ref[...])

    @pl.when(f == 0)
    def _():
        acc_ref[...] = part

    @pl.when(f > 0)
    def _():
        acc_ref[...] += part

    @pl.when(f == pl.num_programs(1) - 1)
    def _():
        o_ref[...] = x_ref[...] + _rms(acc_ref[...], g2_ref[...])


def _ffn(x, g1, w_gu, w_d, g2, tm, tf):
    n, d = x.shape
    fdim = w_d.shape[0]
    nf = fdim // tf
    return pl.pallas_call(
        _ffn_kernel,
        grid=(n // tm, nf),
        in_specs=[
            pl.BlockSpec((tm, d), lambda i, f: (i, 0)),
            pl.BlockSpec((1, d), lambda i, f: (0, 0)),
            pl.BlockSpec((d, tf), lambda i, f: (0, f)),
            pl.BlockSpec((d, tf), lambda i, f: (0, nf + f)),
            pl.BlockSpec((tf, d), lambda i, f: (f, 0)),
            pl.BlockSpec((1, d), lambda i, f: (0, 0)),
        ],
        out_specs=pl.BlockSpec((tm, d), lambda i, f: (i, 0)),
        out_shape=jax.ShapeDtypeStruct((n, d), F32),
        scratch_shapes=[pltpu.VMEM((tm, d), BF16), pltpu.VMEM((tm, d), F32)],
        compiler_params=_cparams("parallel", "arbitrary"),
    )(x, g1.reshape(1, d), w_gu, w_gu, w_d, g2.reshape(1, d))


def _gdn_gate_kernel(ba_ref, nega_ref, dtb_ref, o_ref, *, chunk):
    rows = ba_ref.shape[0]
    lane = lax.broadcasted_iota(jnp.int32, (chunk, GDN_GATE_DIM), 1)
    is_dt = (lane % (2 * GDN_HV)) >= GDN_HV
    is_bwd = lane >= 2 * GDN_HV
    ri = lax.broadcasted_iota(jnp.int32, (chunk, chunk), 0)
    ci = lax.broadcasted_iota(jnp.int32, (chunk, chunk), 1)
    tril = jnp.where(ri >= ci, 1.0, 0.0).astype(BF16)
    triu = jnp.where(ri <= ci, 1.0, 0.0).astype(BF16)
    for c in range(rows // chunk):
        x = ba_ref[c * chunk:(c + 1) * chunk, :]
        beta = 1.0 / (1.0 + jnp.exp(-x))
        y = x + dtb_ref[...]
        sp = jnp.maximum(y, 0.0) + jnp.log1p(jnp.exp(-jnp.abs(y)))
        g = jnp.where(is_dt, nega_ref[...] * sp, 0.0)
        hi = g.astype(BF16)
        lo = (g - hi.astype(F32)).astype(BF16)
        pre = _dot(tril, hi) + _dot(tril, lo)
        suf = _dot(triu, hi) + _dot(triu, lo)
        o_ref[c * chunk:(c + 1) * chunk, :] = jnp.where(is_dt, jnp.where(is_bwd, suf, pre), beta)


def _gdn_gates(ba, a_log, dt_bias, chunk, tt):
    n = ba.shape[0]
    zeros = jnp.zeros((2, GDN_HV), F32)
    nega = jnp.stack([zeros, -jnp.exp(a_log.astype(F32))], axis=1).reshape(1, GDN_GATE_DIM)
    dtb = jnp.stack([zeros, dt_bias.astype(F32)], axis=1).reshape(1, GDN_GATE_DIM)
    return pl.pallas_call(
        functools.partial(_gdn_gate_kernel, chunk=chunk),
        grid=(n // tt,),
        in_specs=[pl.BlockSpec((tt, GDN_GATE_DIM), lambda i: (i, 0)),
                  pl.BlockSpec((1, GDN_GATE_DIM), lambda i: (0, 0)),
                  pl.BlockSpec((1, GDN_GATE_DIM), lambda i: (0, 0))],
        out_specs=pl.BlockSpec((tt, GDN_GATE_DIM), lambda i: (i, 0)),
        out_shape=jax.ShapeDtypeStruct((n, GDN_GATE_DIM), F32),
        compiler_params=_cparams("parallel"),
    )(ba, nega, dtb)


def _gdn_prep_kernel(cur_ref, prev_ref, next_ref, cw_ref, o_ref, ext_ref):
    i = pl.program_id(1)
    j = pl.program_id(2)
    tt, tc = cur_ref.shape[1], cur_ref.shape[2]
    halo = prev_ref.shape[1]
    pad = GDN_CONV_W // 2
    ext_ref[0:halo, :] = jnp.where(i > 0, prev_ref[0].astype(F32), 0.0)
    ext_ref[halo:halo + tt, :] = cur_ref[0].astype(F32)
    ext_ref[halo + tt:2 * halo + tt, :] = jnp.where(i < pl.num_programs(1) - 1, next_ref[0].astype(F32), 0.0)
    acc = ext_ref[pl.ds(halo - pad, tt), :] * cw_ref[0:1, :]
    for w in range(1, GDN_CONV_W):
        acc = acc + ext_ref[pl.ds(halo - pad + w, tt), :] * cw_ref[w:w + 1, :]
    y = _silu(acc)
    col0 = j * tc
    is_q = col0 < GDN_QK_DIM
    is_qk = col0 < 2 * GDN_QK_DIM
    for h in range(tc // GDN_DK):
        sl = slice(h * GDN_DK, (h + 1) * GDN_DK)
        yh = y[:, sl]
        inv = lax.rsqrt(jnp.sum(yh * yh, axis=-1, keepdims=True) + 1e-6)
        f = jnp.where(is_q, inv * GDN_DK ** -0.5, jnp.where(is_qk, inv, 1.0))
        o_ref[0, :, sl] = (yh * f).astype(o_ref.dtype)


def _gdn_prep(proj, conv_w, tt, tc):
    b, t, _ = proj.shape
    halo = BF16_SUBLANE
    nh = t // halo
    r = tt // halo
    return pl.pallas_call(
        _gdn_prep_kernel,
        grid=(b, t // tt, GDN_CONV_DIM // tc),
        in_specs=[
            pl.BlockSpec((1, tt, tc), lambda bi, i, j: (bi, i, j)),
            pl.BlockSpec((1, halo, tc), lambda bi, i, j: (bi, jnp.maximum(i * r - 1, 0), j)),
            pl.BlockSpec((1, halo, tc), lambda bi, i, j: (bi, jnp.minimum((i + 1) * r, nh - 1), j)),
            pl.BlockSpec((GDN_CONV_W, tc), lambda bi, i, j: (0, j)),
        ],
        out_specs=pl.BlockSpec((1, tt, tc), lambda bi, i, j: (bi, i, j)),
        out_shape=jax.ShapeDtypeStruct((b, t, GDN_CONV_DIM), BF16),
        scratch_shapes=[pltpu.VMEM((tt + 2 * halo, tc), F32)],
        compiler_params=_cparams("parallel", "parallel", "parallel"),
    )(proj, proj, proj, conv_w)


def _tri_inverse_masks(c, upper):
    ri = lax.broadcasted_iota(jnp.int32, (c, c), 0)
    ci = lax.broadcasted_iota(jnp.int32, (c, c), 1)
    base_mask = (ri // GDN_INV_BASE) == (ci // GDN_INV_BASE)
    level_masks = []
    b = GDN_INV_BASE
    while b < c:
        mask = ((ri // (2 * b)) == (ci // (2 * b))) & ((ri // b) != (ci // b))
        level_masks.append(_take_blocks(mask, b, 0 if upper else 1))
        b *= 2
    return base_mask, level_masks


def _take_blocks(x, b, parity):
    return jnp.concatenate([x[s:s + b] for s in range(parity * b, x.shape[0], 2 * b)], axis=0)


def _merge_level(p, low, mask, b, upper):
    c = p.shape[0]
    par = 0 if upper else 1
    nblk = c // (2 * b)
    pb = p.astype(BF16)
    off = jnp.where(mask, _take_blocks(low, b, par), 0.0).astype(BF16)
    x = _dot(off, pb).astype(BF16)
    yield
    zero = jnp.zeros((b, c), BF16)
    pieces = []
    for i in range(nblk):
        xi = x[i * b:(i + 1) * b]
        pieces += [xi, zero] if upper else [zero, xi]
    y = _dot(_take_blocks(pb, b, par), jnp.concatenate(pieces, axis=0))
    yield
    rows = []
    for i in range(nblk):
        keep = p[(2 * i + 1 - par) * b:(2 * i + 2 - par) * b]
        upd = p[(2 * i + par) * b:(2 * i + par + 1) * b] - y[i * b:(i + 1) * b]
        rows += [upd, keep] if upper else [keep, upd]
    yield jnp.concatenate(rows, axis=0)


def _unit_tri_inverse(low, eye, masks, upper):
    c = low.shape[0]
    base_mask, level_masks = masks
    m0 = jnp.where(base_mask, -low, 0.0)
    p = eye + m0
    mb = m0.astype(BF16)
    m = _dot(mb, mb)
    yield
    steps = int(math.log2(GDN_INV_BASE))
    for s in range(2, steps + 1):
        mb = m.astype(BF16)
        if s < steps:
            r = _dot(mb, jnp.concatenate([mb, p.astype(BF16)], axis=1))
            m = r[:, :c]
            p = p + r[:, c:]
        else:
            p = p + _dot(mb, p.astype(BF16))
        yield
    b = GDN_INV_BASE
    for mask in level_masks:
        for out in _merge_level(p, low, mask, b, upper):
            if out is None:
                yield
        p = out
        b *= 2
    return p


def _delta_chain(qf, kf, kt, qk, kk, vf, s, beta, gcum, gcum_r, glast, eye, incl, strict, masks, upper):
    chunk = qf.shape[0]
    dec = jnp.where(incl, jnp.exp(gcum - gcum_r), 0.0)
    low = jnp.where(strict, kk * dec, 0.0) * beta
    tinv = yield from _unit_tri_inverse(low, eye, masks, upper)
    eg = jnp.exp(gcum)
    rhs = jnp.concatenate([vf * beta, kf * beta * eg], axis=1).astype(BF16)
    sol = _dot(tinv.astype(BF16), rhs)
    yield
    u, w = sol[:, :GDN_DV], sol[:, GDN_DV:]
    ws_qs = _dot(jnp.concatenate([w, qf * eg], axis=0).astype(BF16), s.astype(BF16))
    yield
    v_new = (u - ws_qs[:chunk]).astype(BF16)
    kend_t = kt * jnp.exp(glast - gcum_r)
    r = _dot(jnp.concatenate([(qk * dec).astype(BF16), kend_t.astype(BF16)], axis=0), v_new)
    yield
    yield ws_qs[chunk:] + r[:chunk], s * jnp.exp(glast) + r[chunk:]


def _run_interleaved(gens):
    results = [None] * len(gens)
    live = list(range(len(gens)))
    while live:
        nxt = []
        for i in live:
            try:
                results[i] = next(gens[i])
                nxt.append(i)
            except StopIteration:
                pass
        live = nxt
    return results


def _gdn_core_kernel(q_ref, k_ref, v_ref, gc_ref, gr_ref, o_ref, s_ref, *, group, chunk):
    t = q_ref.shape[1]
    nc = t // chunk
    ri = lax.broadcasted_iota(jnp.int32, (chunk, chunk), 0)
    ci = lax.broadcasted_iota(jnp.int32, (chunk, chunk), 1)
    eye = jnp.where(ri == ci, 1.0, 0.0).astype(F32)
    incl = (ri >= ci, ri <= ci)
    strict = (ri > ci, ri < ci)
    inv_masks = (_tri_inverse_masks(chunk, False), _tri_inverse_masks(chunk, True))
    s_ref[...] = jnp.zeros(s_ref.shape, F32)
    o_ref[...] = jnp.zeros(o_ref.shape, F32)

    def body(c, carry):
        chains = []
        rows = []
        for d in range(2):
            cc = c if d == 0 else nc - 1 - c
            rows.append(pl.ds(pl.multiple_of(cc * chunk, chunk), chunk))
            gcol = gc_ref[0, 0, rows[d], :]
            grow = gr_ref[0, 0, :, rows[d]]
            edge = chunk - 1 if d == 0 else 0
            for kh in range(group // 2):
                ksl = slice(kh * GDN_DK, (kh + 1) * GDN_DK)
                qb = q_ref[0, rows[d], ksl]
                kb = k_ref[0, rows[d], ksl]
                qf = qb.astype(F32)
                kf = kb.astype(F32)
                gram = _dot_nt(jnp.concatenate([qb, kb], axis=0), kb)
                kt = kf.T
                for j in (2 * kh, 2 * kh + 1):
                    cb = d * 2 * group + j
                    cg = cb + group
                    vf = v_ref[0, rows[d], j * GDN_DV:(j + 1) * GDN_DV].astype(F32)
                    chains.append(_delta_chain(
                        qf, kf, kt, gram[:chunk], gram[chunk:], vf, s_ref[d * group + j],
                        gcol[:, cb:cb + 1], gcol[:, cg:cg + 1], grow[cg:cg + 1, :], gcol[edge:edge + 1, cg:cg + 1],
                        eye, incl[d], strict[d], inv_masks[d], d == 1))
        outs = _run_interleaved(chains)
        for idx, (o_rows, s_new) in enumerate(outs):
            d, j = divmod(idx, group)
            o_ref[0, rows[d], j * GDN_DV:(j + 1) * GDN_DV] += o_rows
            s_ref[d * group + j] = s_new
        return carry

    lax.fori_loop(0, nc, body, 0)


def _gdn_core(qkv, gcol, grow, group, chunk):
    b, t, _ = qkv.shape
    ng = GDN_HV // group
    qw = (group // 2) * GDN_DK
    vw = group * GDN_DV
    kblk0 = GDN_QK_DIM // qw
    vblk0 = 2 * GDN_QK_DIM // vw
    return pl.pallas_call(
        functools.partial(_gdn_core_kernel, group=group, chunk=chunk),
        grid=(b, ng),
        in_specs=[
            pl.BlockSpec((1, t, qw), lambda bi, hg: (bi, 0, hg)),
            pl.BlockSpec((1, t, qw), lambda bi, hg: (bi, 0, kblk0 + hg)),
            pl.BlockSpec((1, t, vw), lambda bi, hg: (bi, 0, vblk0 + hg)),
            pl.BlockSpec((1, 1, t, 4 * group), lambda bi, hg: (bi, hg, 0, 0)),
            pl.BlockSpec((1, 1, 4 * group, t), lambda bi, hg: (bi, hg, 0, 0)),
        ],
        out_specs=pl.BlockSpec((1, t, vw), lambda bi, hg: (bi, 0, hg)),
        out_shape=jax.ShapeDtypeStruct((b, t, GDN_V_DIM), F32),
        scratch_shapes=[pltpu.VMEM((2 * group, GDN_DK, GDN_DV), F32)],
        compiler_params=_cparams("parallel", "parallel"),
    )(qkv, qkv, qkv, gcol, grow)


def _gated_deltanet_layer(x, p, j, li):
    b, t, d = x.shape
    n = b * t
    x2 = x.reshape(n, d)
    g_pre = p['pre_mix_g'][li]
    proj = _norm_matmul(x2, g_pre, p['gdn_w_main'][j], BF16, 1024, 1024)
    ba = _norm_matmul(x2, g_pre, p['gdn_w_gate'][j], F32, 1024, GDN_GATE_DIM)
    gates = _gdn_gates(ba, p['gdn_a_log'][j], p['gdn_dt_bias'][j], GDN_CHUNK, 512)
    grp = GDN_GROUP
    ng = GDN_HV // grp
    gcol = gates.reshape(b, t, 4, ng, grp).transpose(0, 3, 1, 2, 4).reshape(b, ng, t, 4 * grp)
    grow = jnp.swapaxes(gcol, 2, 3)
    qkv = _gdn_prep(proj.reshape(b, t, GDN_MAIN_DIM), p['gdn_conv_w'][j], 512, 512)
    o = _gdn_core(qkv, gcol, grow, grp, GDN_CHUNK)
    y = _matmul_norm_res(o.reshape(n, GDN_V_DIM), p['gdn_w_out'][j], x2, p['post_mix_g'][li], 512, 512,
                         gate=(proj, GDN_CONV_DIM, p['gdn_norm_g'][j]))
    return y.reshape(b, t, d)


def _rel_bucket_table():
    qi = np.arange(SWA_BLOCK)[:, None]
    kj = np.arange(3 * SWA_BLOCK)[None, :]
    rel = kj - SWA_BLOCK - qi
    half = REL_BUCKETS // 2
    max_exact = half // 2
    nabs = np.abs(rel)
    large = max_exact + (np.log(np.maximum(nabs, 1).astype(np.float32) / max_exact)
                         / math.log(REL_MAX_DIST / max_exact) * (half - max_exact)).astype(np.int32)
    large = np.minimum(large, half - 1)
    bucket = np.where(rel > 0, half, 0) + np.where(nabs < max_exact, nabs, large)
    return np.where(nabs <= WINDOW, bucket, -1).astype(np.int32)


def _swa_kernel(q_ref, kp_ref, kc_ref, kn_ref, vp_ref, vc_ref, vn_ref, bucket_ref, relb_ref, sink_ref,
                o_ref, bias_ref):
    b, i, h = pl.program_id(0), pl.program_id(1), pl.program_id(2)
    nb = pl.num_programs(1)
    blk = SWA_BLOCK

    @pl.when((b == 0) & (i == 0) & (h == 0))
    def _():
        bucket = bucket_ref[...]
        for hq in range(SWA_HQ):
            acc = jnp.full(bucket.shape, NEG_BIG, F32)
            for bk in range(REL_BUCKETS):
                acc = jnp.where(bucket == bk, relb_ref[bk, hq], acc)
            bias_ref[hq * blk:(hq + 1) * blk, :] = acc

    rows = SWA_GROUP * blk
    q = q_ref[0]
    qs = jnp.concatenate([q[:, g * SWA_DH:(g + 1) * SWA_DH] for g in range(SWA_GROUP)], axis=0)
    kcat = jnp.concatenate([kp_ref[0], kc_ref[0], kn_ref[0]], axis=0)
    vcat = jnp.concatenate([vp_ref[0], vc_ref[0], vn_ref[0]], axis=0)
    s = _dot_nt(qs, kcat) * SWA_DH ** -0.5 + bias_ref[pl.ds(pl.multiple_of(h * rows, rows), rows), :]
    col = lax.broadcasted_iota(jnp.int32, s.shape, 1)
    outside = ((col < blk) & (i == 0)) | ((col >= 2 * blk) & (i == nb - 1))
    s = jnp.where(outside, NEG_BIG, s)
    sink = jnp.concatenate([jnp.full((blk, 1), sink_ref[h * SWA_GROUP + g], F32) for g in range(SWA_GROUP)], axis=0)
    m = jnp.maximum(jnp.max(s, axis=-1, keepdims=True), sink)
    pexp = jnp.exp(s - m)
    denom = jnp.sum(pexp, axis=-1, keepdims=True) + jnp.exp(sink - m)
    o = _dot(pexp.astype(BF16), vcat) / denom
    for g in range(SWA_GROUP):
        o_ref[0, :, g * SWA_DH:(g + 1) * SWA_DH] = o[g * blk:(g + 1) * blk].astype(o_ref.dtype)


def _swa_attention(qkv, rel_bias, sink):
    b, t, _ = qkv.shape
    nb = t // SWA_BLOCK
    gw = SWA_GROUP * SWA_DH
    k0 = SWA_HQ
    v0 = SWA_HQ + SWA_HKV
    kv_spec = lambda off, blk0: pl.BlockSpec(
        (1, SWA_BLOCK, SWA_DH), lambda bi, i, h: (bi, jnp.clip(i + off, 0, nb - 1), blk0 + h))
    return pl.pallas_call(
        _swa_kernel,
        grid=(b, nb, SWA_HKV),
        in_specs=[
            pl.BlockSpec((1, SWA_BLOCK, gw), lambda bi, i, h: (bi, i, h)),
            kv_spec(-1, k0), kv_spec(0, k0), kv_spec(1, k0),
            kv_spec(-1, v0), kv_spec(0, v0), kv_spec(1, v0),
            pl.BlockSpec((SWA_BLOCK, 3 * SWA_BLOCK), lambda bi, i, h: (0, 0)),
            pl.BlockSpec(memory_space=pltpu.SMEM),
            pl.BlockSpec(memory_space=pltpu.SMEM),
        ],
        out_specs=pl.BlockSpec((1, SWA_BLOCK, gw), lambda bi, i, h: (bi, i, h)),
        out_shape=jax.ShapeDtypeStruct((b, t, SWA_HQ * SWA_DH), BF16),
        scratch_shapes=[pltpu.VMEM((SWA_HQ * SWA_BLOCK, 3 * SWA_BLOCK), F32)],
        compiler_params=_cparams("arbitrary", "arbitrary", "arbitrary"),
    )(qkv, qkv, qkv, qkv, qkv, qkv, qkv, jnp.asarray(_rel_bucket_table()), rel_bias.astype(F32),
      sink.astype(F32))


def _swa_layer(x, p, j, li):
    b, t, d = x.shape
    n = b * t
    x2 = x.reshape(n, d)
    qkv = _norm_matmul(x2, p['pre_mix_g'][li], p['swa_w_in'][j], BF16, 1024, 1024)
    o = _swa_attention(qkv.reshape(b, t, -1), p['rel_bias'], p['swa_sink'][j])
    y = _matmul_norm_res(o.reshape(n, d), p['swa_w_out'][j], x2, p['post_mix_g'][li], 512, 512)
    return y.reshape(b, t, d)


def _fnet_chan_kernel(x_ref, g_ref, w_ref, o_ref, h_ref):
    gi = pl.program_id(1)

    @pl.when(gi == 0)
    def _():
        h_ref[...] = _rms(x_ref[...], g_ref[...]).astype(BF16)

    hg = h_ref[:, pl.ds(pl.multiple_of(gi * FNET_GW, FNET_GW), FNET_GW)]
    r = _dot(hg, w_ref[...])
    o_ref[0, 0] = r[:, :FNET_GW].astype(o_ref.dtype)
    o_ref[0, 1] = r[:, FNET_GW:].astype(o_ref.dtype)


def _fnet_chan(x, g, w_cs, tm):
    b, t, d = x.shape
    tm = min(tm, t)
    nt = t // tm
    return pl.pallas_call(
        _fnet_chan_kernel,
        grid=(b * nt, FNET_GROUPS),
        in_specs=[
            pl.BlockSpec((tm, d), lambda i, gi: (i, 0)),
            pl.BlockSpec((1, d), lambda i, gi: (0, 0)),
            pl.BlockSpec((FNET_GW, 2 * FNET_GW), lambda i, gi: (0, 0)),
        ],
        out_specs=pl.BlockSpec((1, 2, tm, FNET_GW), lambda i, gi: (i // nt, 0, i % nt, gi)),
        out_shape=jax.ShapeDtypeStruct((b, 2, t, d), BF16),
        scratch_shapes=[pltpu.VMEM((tm, d), BF16)],
        compiler_params=_cparams("parallel", "arbitrary"),
    )(x.reshape(b * t, d), g.reshape(1, d), w_cs)


def _bmm_kernel(a_ref, b_ref, o_ref, acc_ref):
    k = pl.program_id(2)
    part = _dot(a_ref[...], b_ref[0])

    @pl.when(k == 0)
    def _():
        acc_ref[...] = part

    @pl.when(k > 0)
    def _():
        acc_ref[...] += part

    @pl.when(k == pl.num_programs(2) - 1)
    def _():
        o_ref[0] = acc_ref[...].astype(o_ref.dtype)


def _shared_lhs_bmm(a, bm, tm, tk):
    m, kdim = a.shape
    b, _, d = bm.shape
    tm = min(tm, m)
    return pl.pallas_call(
        _bmm_kernel,
        grid=(b, m // tm, kdim // tk),
        in_specs=[
            pl.BlockSpec((tm, tk), lambda bi, i, k: (i, k)),
            pl.BlockSpec((1, tk, d), lambda bi, i, k: (bi, k, 0)),
        ],
        out_specs=pl.BlockSpec((1, tm, d), lambda bi, i, k: (bi, i, 0)),
        out_shape=jax.ShapeDtypeStruct((b, m, d), BF16),
        scratch_shapes=[pltpu.VMEM((tm, d), F32)],
        compiler_params=_cparams("parallel", "parallel", "arbitrary"),
    )(a, bm)


def _dft_tables(t):
    def angle(n):
        idx = jnp.arange(n, dtype=jnp.int32)
        return (idx[:, None] * idx[None, :] % n).astype(F32) * (2.0 * math.pi / n)

    ac = angle(FNET_GW)
    at = angle(t)
    scale = 1.0 / math.sqrt(t * FNET_GW)
    w_cs = jnp.concatenate([jnp.cos(ac), jnp.sin(ac)], axis=1).astype(BF16)
    f_t = (jnp.concatenate([jnp.cos(at), -jnp.sin(at)], axis=1) * scale).astype(BF16)
    return w_cs, f_t


def _fnet_layer(x, p, j, li):
    b, t, d = x.shape
    n = b * t
    w_cs, f_t = _dft_tables(t)
    pcs = _fnet_chan(x, p['pre_mix_g'][li], w_cs, 1024)
    y = _shared_lhs_bmm(f_t, pcs.reshape(b, 2 * t, d), 1024, 1024)
    out = _matmul_norm_res(y.reshape(n, d), p['fnet_w_out'][j], x.reshape(n, d), p['post_mix_g'][li], 512, 512)
    return out.reshape(b, t, d)


def _trunk(x, p):
    b, t, d = x.shape
    for i in range(DEPTH):
        kind, j = i % N_MIXERS, i // N_MIXERS
        if kind == 0:
            x = _gated_deltanet_layer(x, p, j, i)
        elif kind == 1:
            x = _swa_layer(x, p, j, i)
        else:
            x = _fnet_layer(x, p, j, i)
        x = _ffn(x.reshape(b * t, d), p['pre_ffn_g'][i], p['ffn_w_gate_up'][i], p['ffn_w_down'][i],
                 p['post_ffn_g'][i], 512, 512).reshape(b, t, d)
    return x


def kernel(x_prompt, x_sample, pre_mix_g, post_mix_g, pre_ffn_g, post_ffn_g, gdn_w_in, gdn_conv_w, gdn_a_log,
           gdn_dt_bias, gdn_norm_g, gdn_w_out, swa_w_in, swa_sink, swa_w_out, rel_bias, fnet_w_out,
           ffn_w_gate_up, ffn_w_down):
    p = dict(
        pre_mix_g=pre_mix_g, post_mix_g=post_mix_g, pre_ffn_g=pre_ffn_g, post_ffn_g=post_ffn_g,
        gdn_w_main=gdn_w_in[:, :, :GDN_MAIN_DIM].astype(BF16),
        gdn_w_gate=gdn_w_in[:, :, GDN_MAIN_DIM:].astype(BF16),
        gdn_conv_w=gdn_conv_w, gdn_a_log=gdn_a_log, gdn_dt_bias=gdn_dt_bias, gdn_norm_g=gdn_norm_g,
        gdn_w_out=gdn_w_out.astype(BF16),
        swa_w_in=swa_w_in.astype(BF16), swa_sink=swa_sink, swa_w_out=swa_w_out.astype(BF16),
        rel_bias=rel_bias, fnet_w_out=fnet_w_out.astype(BF16),
        ffn_w_gate_up=ffn_w_gate_up.astype(BF16), ffn_w_down=ffn_w_down.astype(BF16),
    )
    return _trunk(x_prompt, p), _trunk(x_sample, p)
```

```python
import functools
import math

import numpy as np
import jax
import jax.numpy as jnp
from jax import lax
from jax.experimental import pallas as pl
from jax.experimental.pallas import tpu as pltpu

D_MODEL = 2048
DEPTH = 4
N_MIXERS = 3

GDN_DK = 128
GDN_DV = 128
GDN_HK = D_MODEL // 128
GDN_HV = 2 * GDN_HK
GDN_QK_DIM = GDN_HK * GDN_DK
GDN_V_DIM = GDN_HV * GDN_DV
GDN_CONV_DIM = 2 * GDN_QK_DIM + GDN_V_DIM
GDN_CONV_W = 5
GDN_MAIN_DIM = GDN_CONV_DIM + GDN_V_DIM
GDN_GATE_DIM = 4 * GDN_HV
GDN_CHUNK = 128
GDN_GROUP = 8
GDN_INV_BASE = 16

SWA_DH = 128
SWA_HQ = D_MODEL // SWA_DH
SWA_HKV = 4
SWA_GROUP = SWA_HQ // SWA_HKV
WINDOW = 128
SWA_BLOCK = 128
REL_BUCKETS = 32
REL_MAX_DIST = 128

FNET_GROUPS = 4
FNET_GW = D_MODEL // FNET_GROUPS

FFN_HIDDEN = -(-8 * D_MODEL // (3 * 256)) * 256

RMS_EPS = 1e-6
NEG_BIG = -1e30

LANE = 128
BF16_SUBLANE = 16
VMEM_LIMIT_BYTES = 56 * 1024 * 1024

BF16 = jnp.bfloat16
F32 = jnp.float32


def _cparams(*sem):
    return pltpu.CompilerParams(dimension_semantics=sem, vmem_limit_bytes=VMEM_LIMIT_BYTES)


def _rms(x, g):
    return x * lax.rsqrt(jnp.mean(x * x, axis=-1, keepdims=True) + RMS_EPS) * g


def _silu(x):
    return x * (1.0 / (1.0 + jnp.exp(-x)))


def _dot(a, b):
    return jnp.dot(a, b, preferred_element_type=F32)


def _dot_nt(a, b):
    return lax.dot_general(a, b, (((1,), (1,)), ((), ())), preferred_element_type=F32)


def _norm_matmul_kernel(x_ref, g_ref, w_ref, o_ref, h_ref):
    @pl.when(pl.program_id(1) == 0)
    def _():
        h_ref[...] = _rms(x_ref[...], g_ref[...]).astype(h_ref.dtype)

    o_ref[...] = _dot(h_ref[...], w_ref[...]).astype(o_ref.dtype)


def _norm_matmul(x, g, w, out_dtype, tm, tn):
    n, d = x.shape
    m = w.shape[1]
    tm = min(tm, n)
    return pl.pallas_call(
        _norm_matmul_kernel,
        grid=(n // tm, m // tn),
        in_specs=[
            pl.BlockSpec((tm, d), lambda i, j: (i, 0)),
            pl.BlockSpec((1, d), lambda i, j: (0, 0)),
            pl.BlockSpec((d, tn), lambda i, j: (0, j)),
        ],
        out_specs=pl.BlockSpec((tm, tn), lambda i, j: (i, j)),
        out_shape=jax.ShapeDtypeStruct((n, m), out_dtype),
        scratch_shapes=[pltpu.VMEM((tm, d), BF16)],
        compiler_params=_cparams("parallel", "arbitrary"),
    )(x, g.reshape(1, d), w)


def _matmul_norm_res_kernel(*refs, gated):
    if gated:
        a_ref, z_ref, ng_ref, w_ref, x_ref, g_ref, o_ref = refs
        ng = ng_ref[...]
        parts = []
        for h in range(a_ref.shape[1] // GDN_DV):
            sl = slice(h * GDN_DV, (h + 1) * GDN_DV)
            parts.append((_rms(a_ref[:, sl], ng) * _silu(z_ref[:, sl].astype(F32))).astype(BF16))
        a = jnp.concatenate(parts, axis=1)
    else:
        a_ref, w_ref, x_ref, g_ref, o_ref = refs
        a = a_ref[...].astype(BF16)
    o_ref[...] = x_ref[...] + _rms(_dot(a, w_ref[...]), g_ref[...])


def _matmul_norm_res(a, w, x, g, tm, gate=None):
    n, kdim = a.shape
    d = w.shape[1]
    in_specs = [pl.BlockSpec((tm, kdim), lambda i: (i, 0))]
    args = [a]
    if gate is not None:
        zsrc, zcol0, norm_g = gate
        zblk = zcol0 // kdim
        in_specs += [pl.BlockSpec((tm, kdim), lambda i: (i, zblk)),
                     pl.BlockSpec((1, GDN_DV), lambda i: (0, 0))]
        args += [zsrc, norm_g.reshape(1, GDN_DV)]
    in_specs += [pl.BlockSpec((kdim, d), lambda i: (0, 0), pipeline_mode=pl.Buffered(1)),
                 pl.BlockSpec((tm, d), lambda i: (i, 0)),
                 pl.BlockSpec((1, d), lambda i: (0, 0))]
    args += [w, x, g.reshape(1, d)]
    return pl.pallas_call(
        functools.partial(_matmul_norm_res_kernel, gated=gate is not None),
        grid=(n // tm,),
        in_specs=in_specs,
        out_specs=pl.BlockSpec((tm, d), lambda i: (i, 0)),
        out_shape=jax.ShapeDtypeStruct((n, d), F32),
        compiler_params=_cparams("parallel"),
    )(*args)


def _ffn_kernel(x_ref, g1_ref, wg_ref, wu_ref, wd_ref, g2_ref, o_ref, h_ref, acc_ref):
    f = pl.program_id(1)

    @pl.when(f == 0)
    def _():
        h_ref[...] = _rms(x_ref[...], g1_ref[...]).astype(BF16)

    h = h_ref[...]
    gate = _dot(h, wg_ref[...])
    up = _dot(h, wu_ref[...])
    part = _dot((_silu(gate) * up).astype(BF16), wd_ref[...])

    @pl.when(f == 0)
    def _():
        acc_ref[...] = part

    @pl.when(f > 0)
    def _():
        acc_ref[...] += part

    @pl.when(f == pl.num_programs(1) - 1)
    def _():
        o_ref[...] = x_ref[...] + _rms(acc_ref[...], g2_ref[...])


def _ffn(x, g1, w_gu, w_d, g2, tm, tf):
    n, d = x.shape
    fdim = w_d.shape[0]
    nf = fdim // tf
    tm = min(tm, n)
    return pl.pallas_call(
        _ffn_kernel,
        grid=(n // tm, nf),
        in_specs=[
            pl.BlockSpec((tm, d), lambda i, f: (i, 0), pipeline_mode=pl.Buffered(1)),
            pl.BlockSpec((1, d), lambda i, f: (0, 0)),
            pl.BlockSpec((d, tf), lambda i, f: (0, f)),
            pl.BlockSpec((d, tf), lambda i, f: (0, nf + f)),
            pl.BlockSpec((tf, d), lambda i, f: (f, 0)),
            pl.BlockSpec((1, d), lambda i, f: (0, 0)),
        ],
        out_specs=pl.BlockSpec((tm, d), lambda i, f: (i, 0), pipeline_mode=pl.Buffered(1)),
        out_shape=jax.ShapeDtypeStruct((n, d), F32),
        scratch_shapes=[pltpu.VMEM((tm, d), BF16), pltpu.VMEM((tm, d), F32)],
        compiler_params=_cparams("parallel", "arbitrary"),
    )(x, g1.reshape(1, d), w_gu, w_gu, w_d, g2.reshape(1, d))


def _gdn_gate_kernel(ba_ref, nega_ref, dtb_ref, o_ref, *, chunk):
    rows = ba_ref.shape[0]
    lane = lax.broadcasted_iota(jnp.int32, (chunk, GDN_GATE_DIM), 1)
    is_dt = (lane % (2 * GDN_HV)) >= GDN_HV
    is_bwd = lane >= 2 * GDN_HV
    ri = lax.broadcasted_iota(jnp.int32, (chunk, chunk), 0)
    ci = lax.broadcasted_iota(jnp.int32, (chunk, chunk), 1)
    tril = jnp.where(ri >= ci, 1.0, 0.0).astype(BF16)
    triu = jnp.where(ri <= ci, 1.0, 0.0).astype(BF16)
    for c in range(rows // chunk):
        x = ba_ref[c * chunk:(c + 1) * chunk, :]
        beta = 1.0 / (1.0 + jnp.exp(-x))
        y = x + dtb_ref[...]
        sp = jnp.maximum(y, 0.0) + jnp.log1p(jnp.exp(-jnp.abs(y)))
        g = jnp.where(is_dt, nega_ref[...] * sp, 0.0)
        hi = g.astype(BF16)
        lo = (g - hi.astype(F32)).astype(BF16)
        pre = _dot(tril, hi) + _dot(tril, lo)
        suf = _dot(triu, hi) + _dot(triu, lo)
        o_ref[c * chunk:(c + 1) * chunk, :] = jnp.where(is_dt, jnp.where(is_bwd, suf, pre), beta)


def _gdn_gates(ba, a_log, dt_bias, chunk, tt):
    n = ba.shape[0]
    zeros = jnp.zeros((2, GDN_HV), F32)
    nega = jnp.stack([zeros, -jnp.exp(a_log.astype(F32))], axis=1).reshape(1, GDN_GATE_DIM)
    dtb = jnp.stack([zeros, dt_bias.astype(F32)], axis=1).reshape(1, GDN_GATE_DIM)
    return pl.pallas_call(
        functools.partial(_gdn_gate_kernel, chunk=chunk),
        grid=(n // tt,),
        in_specs=[pl.BlockSpec((tt, GDN_GATE_DIM), lambda i: (i, 0)),
                  pl.BlockSpec((1, GDN_GATE_DIM), lambda i: (0, 0)),
                  pl.BlockSpec((1, GDN_GATE_DIM), lambda i: (0, 0))],
        out_specs=pl.BlockSpec((tt, GDN_GATE_DIM), lambda i: (i, 0)),
        out_shape=jax.ShapeDtypeStruct((n, GDN_GATE_DIM), F32),
        compiler_params=_cparams("parallel"),
    )(ba, nega, dtb)


def _gdn_prep_kernel(cur_ref, prev_ref, next_ref, cw_ref, o_ref, ext_ref):
    i = pl.program_id(1)
    j = pl.program_id(2)
    tt, tc = cur_ref.shape[1], cur_ref.shape[2]
    halo = prev_ref.shape[1]
    pad = GDN_CONV_W // 2
    ext_ref[0:halo, :] = jnp.where(i > 0, prev_ref[0].astype(F32), 0.0)
    ext_ref[halo:halo + tt, :] = cur_ref[0].astype(F32)
    ext_ref[halo + tt:2 * halo + tt, :] = jnp.where(i < pl.num_programs(1) - 1, next_ref[0].astype(F32), 0.0)
    acc = ext_ref[pl.ds(halo - pad, tt), :] * cw_ref[0:1, :]
    for w in range(1, GDN_CONV_W):
        acc = acc + ext_ref[pl.ds(halo - pad + w, tt), :] * cw_ref[w:w + 1, :]
    y = _silu(acc)
    col0 = j * tc
    is_qk = col0 < 2 * GDN_QK_DIM

    @pl.when(is_qk)
    def _():
        qscale = jnp.where(col0 < GDN_QK_DIM, GDN_DK ** -0.5, 1.0)
        for h in range(tc // GDN_DK):
            sl = slice(h * GDN_DK, (h + 1) * GDN_DK)
            yh = y[:, sl]
            inv = lax.rsqrt(jnp.sum(yh * yh, axis=-1, keepdims=True) + 1e-6)
            o_ref[0, :, sl] = (yh * inv * qscale).astype(o_ref.dtype)

    @pl.when(jnp.logical_not(is_qk))
    def _():
        o_ref[0] = y.astype(o_ref.dtype)


def _gdn_prep(proj, conv_w, tt, tc):
    b, t, _ = proj.shape
    halo = BF16_SUBLANE
    nh = t // halo
    r = tt // halo
    return pl.pallas_call(
        _gdn_prep_kernel,
        grid=(b, t // tt, GDN_CONV_DIM // tc),
        in_specs=[
            pl.BlockSpec((1, tt, tc), lambda bi, i, j: (bi, i, j)),
            pl.BlockSpec((1, halo, tc), lambda bi, i, j: (bi, jnp.maximum(i * r - 1, 0), j)),
            pl.BlockSpec((1, halo, tc), lambda bi, i, j: (bi, jnp.minimum((i + 1) * r, nh - 1), j)),
            pl.BlockSpec((GDN_CONV_W, tc), lambda bi, i, j: (0, j)),
        ],
        out_specs=pl.BlockSpec((1, tt, tc), lambda bi, i, j: (bi, i, j)),
        out_shape=jax.ShapeDtypeStruct((b, t, GDN_CONV_DIM), BF16),
        scratch_shapes=[pltpu.VMEM((tt + 2 * halo, tc), F32)],
        compiler_params=_cparams("parallel", "parallel", "parallel"),
    )(proj, proj, proj, conv_w)


def _tri_inverse_masks(c, upper):
    ri = lax.broadcasted_iota(jnp.int32, (c, c), 0)
    ci = lax.broadcasted_iota(jnp.int32, (c, c), 1)
    base_mask = (ri // GDN_INV_BASE) == (ci // GDN_INV_BASE)
    level_masks = []
    b = GDN_INV_BASE
    while b < c:
        mask = ((ri // (2 * b)) == (ci // (2 * b))) & ((ri // b) != (ci // b))
        level_masks.append(_take_blocks(mask, b, 0 if upper else 1))
        b *= 2
    return base_mask, level_masks


def _take_blocks(x, b, parity):
    return jnp.concatenate([x[s:s + b] for s in range(parity * b, x.shape[0], 2 * b)], axis=0)


def _merge_level(p, low, mask, b, upper):
    c = p.shape[0]
    par = 0 if upper else 1
    nblk = c // (2 * b)
    pb = p.astype(BF16)
    off = jnp.where(mask, _take_blocks(low, b, par), 0.0).astype(BF16)
    x = _dot(off, pb).astype(BF16)
    yield
    zero = jnp.zeros((b, c), BF16)
    pieces = []
    for i in range(nblk):
        xi = x[i * b:(i + 1) * b]
        pieces += [xi, zero] if upper else [zero, xi]
    y = _dot(_take_blocks(pb, b, par), jnp.concatenate(pieces, axis=0))
    yield
    rows = []
    for i in range(nblk):
        keep = p[(2 * i + 1 - par) * b:(2 * i + 2 - par) * b]
        upd = p[(2 * i + par) * b:(2 * i + par + 1) * b] - y[i * b:(i + 1) * b]
        rows += [upd, keep] if upper else [keep, upd]
    yield jnp.concatenate(rows, axis=0)


def _unit_tri_inverse(low, eye, masks, upper):
    c = low.shape[0]
    base_mask, level_masks = masks
    m0 = jnp.where(base_mask, -low, 0.0)
    p = eye + m0
    mb = m0.astype(BF16)
    m = _dot(mb, mb)
    yield
    steps = int(math.log2(GDN_INV_BASE))
    for s in range(2, steps + 1):
        mb = m.astype(BF16)
        if s < steps:
            r = _dot(mb, jnp.concatenate([mb, p.astype(BF16)], axis=1))
            m = r[:, :c]
            p = p + r[:, c:]
        else:
            p = p + _dot(mb, p.astype(BF16))
        yield
    b = GDN_INV_BASE
    for mask in level_masks:
        for out in _merge_level(p, low, mask, b, upper):
            if out is None:
                yield
        p = out
        b *= 2
    return p


def _delta_chain(qf, kf, kt, qk, kk, vf, s, beta, gcum, gcum_r, glast, eye, incl, strict, masks, upper):
    chunk = qf.shape[0]
    dec = jnp.where(incl, jnp.exp(gcum - gcum_r), 0.0)
    low = jnp.where(strict, kk * dec, 0.0) * beta
    tinv = yield from _unit_tri_inverse(low, eye, masks, upper)
    eg = jnp.exp(gcum)
    rhs = jnp.concatenate([vf * beta, kf * beta * eg], axis=1).astype(BF16)
    sol = _dot(tinv.astype(BF16), rhs)
    yield
    u, w = sol[:, :GDN_DV], sol[:, GDN_DV:]
    ws_qs = _dot(jnp.concatenate([w, qf * eg], axis=0).astype(BF16), s.astype(BF16))
    yield
    v_new = (u - ws_qs[:chunk]).astype(BF16)
    kend_t = kt * jnp.exp(glast - gcum_r)
    r = _dot(jnp.concatenate([(qk * dec).astype(BF16), kend_t.astype(BF16)], axis=0), v_new)
    yield
    yield ws_qs[chunk:] + r[:chunk], s * jnp.exp(glast) + r[chunk:]


def _run_interleaved(gens):
    results = [None] * len(gens)
    live = list(range(len(gens)))
    while live:
        nxt = []
        for i in live:
            try:
                results[i] = next(gens[i])
                nxt.append(i)
            except StopIteration:
                pass
        live = nxt
    return results


def _gdn_core_kernel(q_ref, k_ref, v_ref, gc_ref, gr_ref, o_ref, s_ref, *, group, chunk):
    t = q_ref.shape[1]
    nc = t // chunk
    ri = lax.broadcasted_iota(jnp.int32, (chunk, chunk), 0)
    ci = lax.broadcasted_iota(jnp.int32, (chunk, chunk), 1)
    eye = jnp.where(ri == ci, 1.0, 0.0).astype(F32)
    incl = (ri >= ci, ri <= ci)
    strict = (ri > ci, ri < ci)
    inv_masks = (_tri_inverse_masks(chunk, False), _tri_inverse_masks(chunk, True))
    s_ref[...] = jnp.zeros(s_ref.shape, F32)
    o_ref[...] = jnp.zeros(o_ref.shape, F32)

    def body(c, carry):
        chains = []
        rows = []
        for d in range(2):
            cc = c if d == 0 else nc - 1 - c
            rows.append(pl.ds(pl.multiple_of(cc * chunk, chunk), chunk))
            gcol = gc_ref[0, 0, rows[d], :]
            grow = gr_ref[0, 0, :, rows[d]]
            edge = chunk - 1 if d == 0 else 0
            for kh in range(group // 2):
                ksl = slice(kh * GDN_DK, (kh + 1) * GDN_DK)
                qb = q_ref[0, rows[d], ksl]
                kb = k_ref[0, rows[d], ksl]
                qf = qb.astype(F32)
                kf = kb.astype(F32)
                gram = _dot_nt(jnp.concatenate([qb, kb], axis=0), kb)
                kt = kf.T
                for j in (2 * kh, 2 * kh + 1):
                    cb = d * 2 * group + j
                    cg = cb + group
                    vf = v_ref[0, rows[d], j * GDN_DV:(j + 1) * GDN_DV].astype(F32)
                    chains.append(_delta_chain(
                        qf, kf, kt, gram[:chunk], gram[chunk:], vf, s_ref[d * group + j],
                        gcol[:, cb:cb + 1], gcol[:, cg:cg + 1], grow[cg:cg + 1, :], gcol[edge:edge + 1, cg:cg + 1],
                        eye, incl[d], strict[d], inv_masks[d], d == 1))
        outs = _run_interleaved(chains)
        for idx, (o_rows, s_new) in enumerate(outs):
            d, j = divmod(idx, group)
            o_ref[0, rows[d], j * GDN_DV:(j + 1) * GDN_DV] += o_rows
            s_ref[d * group + j] = s_new
        return carry

    lax.fori_loop(0, nc, body, 0)


def _gdn_core(qkv, gcol, grow, group, chunk):
    b, t, _ = qkv.shape
    ng = GDN_HV // group
    qw = (group // 2) * GDN_DK
    vw = group * GDN_DV
    kblk0 = GDN_QK_DIM // qw
    vblk0 = 2 * GDN_QK_DIM // vw
    once = pl.Buffered(1)
    return pl.pallas_call(
        functools.partial(_gdn_core_kernel, group=group, chunk=chunk),
        grid=(b, ng),
        in_specs=[
            pl.BlockSpec((1, t, qw), lambda bi, hg: (bi, 0, hg), pipeline_mode=once),
            pl.BlockSpec((1, t, qw), lambda bi, hg: (bi, 0, kblk0 + hg), pipeline_mode=once),
            pl.BlockSpec((1, t, vw), lambda bi, hg: (bi, 0, vblk0 + hg), pipeline_mode=once),
            pl.BlockSpec((1, 1, t, 4 * group), lambda bi, hg: (bi, hg, 0, 0)),
            pl.BlockSpec((1, 1, 4 * group, t), lambda bi, hg: (bi, hg, 0, 0)),
        ],
        out_specs=pl.BlockSpec((1, t, vw), lambda bi, hg: (bi, 0, hg), pipeline_mode=once),
        out_shape=jax.ShapeDtypeStruct((b, t, GDN_V_DIM), F32),
        scratch_shapes=[pltpu.VMEM((2 * group, GDN_DK, GDN_DV), F32)],
        compiler_params=_cparams("parallel", "parallel"),
    )(qkv, qkv, qkv, gcol, grow)


def _gated_deltanet_layer(x, p, j, li):
    b, t, d = x.shape
    n = b * t
    x2 = x.reshape(n, d)
    g_pre = p['pre_mix_g'][li]
    proj = _norm_matmul(x2, g_pre, p['gdn_w_main'][j], BF16, 1024, 1024)
    ba = _norm_matmul(x2, g_pre, p['gdn_w_gate'][j], F32, 1024, GDN_GATE_DIM)
    gates = _gdn_gates(ba, p['gdn_a_log'][j], p['gdn_dt_bias'][j], GDN_CHUNK, 512)
    grp = GDN_GROUP
    ng = GDN_HV // grp
    gcol = gates.reshape(b, t, 4, ng, grp).transpose(0, 3, 1, 2, 4).reshape(b, ng, t, 4 * grp)
    grow = jnp.swapaxes(gcol, 2, 3)
    qkv = _gdn_prep(proj.reshape(b, t, GDN_MAIN_DIM), p['gdn_conv_w'][j], 512, 512)
    o = _gdn_core(qkv, gcol, grow, grp, GDN_CHUNK)
    y = _matmul_norm_res(o.reshape(n, GDN_V_DIM), p['gdn_w_out'][j], x2, p['post_mix_g'][li], 256,
                         gate=(proj, GDN_CONV_DIM, p['gdn_norm_g'][j]))
    return y.reshape(b, t, d)


def _rel_bucket_table():
    qi = np.arange(SWA_BLOCK)[:, None]
    kj = np.arange(3 * SWA_BLOCK)[None, :]
    rel = kj - SWA_BLOCK - qi
    half = REL_BUCKETS // 2
    max_exact = half // 2
    nabs = np.abs(rel)
    large = max_exact + (np.log(np.maximum(nabs, 1).astype(np.float32) / max_exact)
                         / math.log(REL_MAX_DIST / max_exact) * (half - max_exact)).astype(np.int32)
    large = np.minimum(large, half - 1)
    bucket = np.where(rel > 0, half, 0) + np.where(nabs < max_exact, nabs, large)
    return np.where(nabs <= WINDOW, bucket, -1).astype(np.int32)


def _swa_kernel(q_ref, kp_ref, kc_ref, kn_ref, vp_ref, vc_ref, vn_ref, bucket_ref, relb_ref, sink_ref,
                o_ref, bias_ref):
    b, i = pl.program_id(0), pl.program_id(1)
    nb = pl.num_programs(1)
    blk = SWA_BLOCK

    @pl.when((b == 0) & (i == 0))
    def _():
        bucket = bucket_ref[...]
        for hq in range(SWA_HQ):
            acc = jnp.full(bucket.shape, NEG_BIG, F32)
            for bk in range(REL_BUCKETS):
                acc = jnp.where(bucket == bk, relb_ref[bk, hq], acc)
            bias_ref[hq * blk:(hq + 1) * blk, :] = acc

    rows = SWA_GROUP * blk
    col = lax.broadcasted_iota(jnp.int32, (1, 3 * blk), 1)
    outside = ((col < blk) & (i == 0)) | ((col >= 2 * blk) & (i == nb - 1))
    edge = jnp.where(outside, NEG_BIG, 0.0)
    for h in range(SWA_HKV):
        hsl = slice(h * SWA_DH, (h + 1) * SWA_DH)
        qs = jnp.concatenate(
            [q_ref[0, :, (h * SWA_GROUP + g) * SWA_DH:(h * SWA_GROUP + g + 1) * SWA_DH] for g in range(SWA_GROUP)],
            axis=0)
        kcat = jnp.concatenate([kp_ref[0, :, hsl], kc_ref[0, :, hsl], kn_ref[0, :, hsl]], axis=0)
        vcat = jnp.concatenate([vp_ref[0, :, hsl], vc_ref[0, :, hsl], vn_ref[0, :, hsl]], axis=0)
        s = _dot_nt(qs, kcat) * SWA_DH ** -0.5 + (bias_ref[h * rows:(h + 1) * rows, :] + edge)
        sink = jnp.concatenate(
            [jnp.full((blk, 1), sink_ref[h * SWA_GROUP + g], F32) for g in range(SWA_GROUP)], axis=0)
        m = jnp.maximum(jnp.max(s, axis=-1, keepdims=True), sink)
        pexp = jnp.exp(s - m)
        denom = jnp.sum(pexp, axis=-1, keepdims=True) + jnp.exp(sink - m)
        o = _dot(pexp.astype(BF16), vcat) / denom
        for g in range(SWA_GROUP):
            osl = slice((h * SWA_GROUP + g) * SWA_DH, (h * SWA_GROUP + g + 1) * SWA_DH)
            o_ref[0, :, osl] = o[g * blk:(g + 1) * blk].astype(o_ref.dtype)


def _swa_attention(qkv, rel_bias, sink):
    b, t, _ = qkv.shape
    nb = t // SWA_BLOCK
    qw = SWA_HQ * SWA_DH
    kvw = SWA_HKV * SWA_DH
    kblk = qw // kvw
    kv_spec = lambda off, cblk: pl.BlockSpec(
        (1, SWA_BLOCK, kvw), lambda bi, i: (bi, jnp.clip(i + off, 0, nb - 1), cblk))
    return pl.pallas_call(
        _swa_kernel,
        grid=(b, nb),
        in_specs=[
            pl.BlockSpec((1, SWA_BLOCK, qw), lambda bi, i: (bi, i, 0)),
            kv_spec(-1, kblk), kv_spec(0, kblk), kv_spec(1, kblk),
            kv_spec(-1, kblk + 1), kv_spec(0, kblk + 1), kv_spec(1, kblk + 1),
            pl.BlockSpec((SWA_BLOCK, 3 * SWA_BLOCK), lambda bi, i: (0, 0)),
            pl.BlockSpec(memory_space=pltpu.SMEM),
            pl.BlockSpec(memory_space=pltpu.SMEM),
        ],
        out_specs=pl.BlockSpec((1, SWA_BLOCK, qw), lambda bi, i: (bi, i, 0)),
        out_shape=jax.ShapeDtypeStruct((b, t, qw), BF16),
        scratch_shapes=[pltpu.VMEM((SWA_HQ * SWA_BLOCK, 3 * SWA_BLOCK), F32)],
        compiler_params=_cparams("arbitrary", "arbitrary"),
    )(qkv, qkv, qkv, qkv, qkv, qkv, qkv, jnp.asarray(_rel_bucket_table()), rel_bias.astype(F32),
      sink.astype(F32))


def _swa_layer(x, p, j, li):
    b, t, d = x.shape
    n = b * t
    x2 = x.reshape(n, d)
    qkv = _norm_matmul(x2, p['pre_mix_g'][li], p['swa_w_in'][j], BF16, 1024, 1024)
    o = _swa_attention(qkv.reshape(b, t, -1), p['rel_bias'], p['swa_sink'][j])
    y = _matmul_norm_res(o.reshape(n, d), p['swa_w_out'][j], x2, p['post_mix_g'][li], 512)
    return y.reshape(b, t, d)


def _fnet_chan_kernel(x_ref, g_ref, w_ref, o_ref, h_ref):
    gi = pl.program_id(1)

    @pl.when(gi == 0)
    def _():
        h_ref[...] = _rms(x_ref[...], g_ref[...]).astype(BF16)

    hg = h_ref[:, pl.ds(pl.multiple_of(gi * FNET_GW, FNET_GW), FNET_GW)]
    r = _dot(hg, w_ref[...])
    o_ref[0, 0] = r[:, :FNET_GW].astype(o_ref.dtype)
    o_ref[0, 1] = r[:, FNET_GW:].astype(o_ref.dtype)


def _fnet_chan(x, g, w_cs, tm):
    b, t, d = x.shape
    tm = min(tm, t)
    nt = t // tm
    return pl.pallas_call(
        _fnet_chan_kernel,
        grid=(b * nt, FNET_GROUPS),
        in_specs=[
            pl.BlockSpec((tm, d), lambda i, gi: (i, 0)),
            pl.BlockSpec((1, d), lambda i, gi: (0, 0)),
            pl.BlockSpec((FNET_GW, 2 * FNET_GW), lambda i, gi: (0, 0)),
        ],
        out_specs=pl.BlockSpec((1, 2, tm, FNET_GW), lambda i, gi: (i // nt, 0, i % nt, gi)),
        out_shape=jax.ShapeDtypeStruct((b, 2, t, d), BF16),
        scratch_shapes=[pltpu.VMEM((tm, d), BF16)],
        compiler_params=_cparams("parallel", "arbitrary"),
    )(x.reshape(b * t, d), g.reshape(1, d), w_cs)


def _bmm_kernel(a_ref, b_ref, o_ref, acc_ref):
    k = pl.program_id(2)
    part = _dot(a_ref[...], b_ref[0])

    @pl.when(k == 0)
    def _():
        acc_ref[...] = part

    @pl.when(k > 0)
    def _():
        acc_ref[...] += part

    @pl.when(k == pl.num_programs(2) - 1)
    def _():
        o_ref[0] = acc_ref[...].astype(o_ref.dtype)


def _shared_lhs_bmm(a, bm, tm, tk):
    m, kdim = a.shape
    b, _, d = bm.shape
    tm = min(tm, m)
    return pl.pallas_call(
        _bmm_kernel,
        grid=(b, m // tm, kdim // tk),
        in_specs=[
            pl.BlockSpec((tm, tk), lambda bi, i, k: (i, k)),
            pl.BlockSpec((1, tk, d), lambda bi, i, k: (bi, k, 0)),
        ],
        out_specs=pl.BlockSpec((1, tm, d), lambda bi, i, k: (bi, i, 0)),
        out_shape=jax.ShapeDtypeStruct((b, m, d), BF16),
        scratch_shapes=[pltpu.VMEM((tm, d), F32)],
        compiler_params=_cparams("parallel", "parallel", "arbitrary"),
    )(a, bm)


def _dft_tables(t):
    def angle(n):
        idx = jnp.arange(n, dtype=jnp.int32)
        return (idx[:, None] * idx[None, :] % n).astype(F32) * (2.0 * math.pi / n)

    ac = angle(FNET_GW)
    at = angle(t)
    scale = 1.0 / math.sqrt(t * FNET_GW)
    w_cs = jnp.concatenate([jnp.cos(ac), jnp.sin(ac)], axis=1).astype(BF16)
    f_t = (jnp.concatenate([jnp.cos(at), -jnp.sin(at)], axis=1) * scale).astype(BF16)
    return w_cs, f_t


def _fnet_layer(x, p, j, li):
    b, t, d = x.shape
    n = b * t
    w_cs, f_t = _dft_tables(t)
    pcs = _fnet_chan(x, p['pre_mix_g'][li], w_cs, 1024)
    y = _shared_lhs_bmm(f_t, pcs.reshape(b, 2 * t, d), 1024, 1024)
    out = _matmul_norm_res(y.reshape(n, d), p['fnet_w_out'][j], x.reshape(n, d), p['post_mix_g'][li], 512)
    return out.reshape(b, t, d)


def _trunk(x, p):
    b, t, d = x.shape
    for i in range(DEPTH):
        kind, j = i % N_MIXERS, i // N_MIXERS
        if kind == 0:
            x = _gated_deltanet_layer(x, p, j, i)
        elif kind == 1:
            x = _swa_layer(x, p, j, i)
        else:
            x = _fnet_layer(x, p, j, i)
        x = _ffn(x.reshape(b * t, d), p['pre_ffn_g'][i], p['ffn_w_gate_up'][i], p['ffn_w_down'][i],
                 p['post_ffn_g'][i], 1024, 512).reshape(b, t, d)
    return x


def kernel(x_prompt, x_sample, pre_mix_g, post_mix_g, pre_ffn_g, post_ffn_g, gdn_w_in, gdn_conv_w, gdn_a_log,
           gdn_dt_bias, gdn_norm_g, gdn_w_out, swa_w_in, swa_sink, swa_w_out, rel_bias, fnet_w_out,
           ffn_w_gate_up, ffn_w_down):
    p = dict(
        pre_mix_g=pre_mix_g, post_mix_g=post_mix_g, pre_ffn_g=pre_ffn_g, post_ffn_g=post_ffn_g,
        gdn_w_main=gdn_w_in[:, :, :GDN_MAIN_DIM].astype(BF16),
        gdn_w_gate=gdn_w_in[:, :, GDN_MAIN_DIM:].astype(BF16),
        gdn_conv_w=gdn_conv_w, gdn_a_log=gdn_a_log, gdn_dt_bias=gdn_dt_bias, gdn_norm_g=gdn_norm_g,
        gdn_w_out=gdn_w_out.astype(BF16),
        swa_w_in=swa_w_in.astype(BF16), swa_sink=swa_sink, swa_w_out=swa_w_out.astype(BF16),
        rel_bias=rel_bias, fnet_w_out=fnet_w_out.astype(BF16),
        ffn_w_gate_up=ffn_w_gate_up.astype(BF16), ffn_w_down=ffn_w_down.astype(BF16),
    )
    return _trunk(x_prompt, p), _trunk(x_sample, p)
```

```python
import functools
import math

import numpy as np
import jax
import jax.numpy as jnp
from jax import lax
from jax.experimental import pallas as pl
from jax.experimental.pallas import tpu as pltpu

D_MODEL = 2048
DEPTH = 4
N_MIXERS = 3

GDN_DK = 128
GDN_DV = 128
GDN_HK = D_MODEL // 128
GDN_HV = 2 * GDN_HK
GDN_QK_DIM = GDN_HK * GDN_DK
GDN_V_DIM = GDN_HV * GDN_DV
GDN_CONV_DIM = 2 * GDN_QK_DIM + GDN_V_DIM
GDN_CONV_W = 5
GDN_MAIN_DIM = GDN_CONV_DIM + GDN_V_DIM
GDN_GATE_DIM = 4 * GDN_HV
GDN_CHUNK = 128
GDN_GROUP = 8
GDN_INV_BASE = 16

SWA_DH = 128
SWA_HQ = D_MODEL // SWA_DH
SWA_HKV = 4
SWA_GROUP = SWA_HQ // SWA_HKV
WINDOW = 128
SWA_BLOCK = 128
REL_BUCKETS = 32
REL_MAX_DIST = 128

FNET_GROUPS = 4
FNET_GW = D_MODEL // FNET_GROUPS

FFN_HIDDEN = -(-8 * D_MODEL // (3 * 256)) * 256

RMS_EPS = 1e-6
NEG_BIG = -1e30

LANE = 128
BF16_SUBLANE = 16
VMEM_LIMIT_BYTES = 56 * 1024 * 1024

BF16 = jnp.bfloat16
F32 = jnp.float32


def _cparams(*sem):
    return pltpu.CompilerParams(dimension_semantics=sem, vmem_limit_bytes=VMEM_LIMIT_BYTES)


def _rms(x, g):
    return x * lax.rsqrt(jnp.mean(x * x, axis=-1, keepdims=True) + RMS_EPS) * g


def _silu(x):
    return x * (1.0 / (1.0 + jnp.exp(-x)))


def _dot(a, b):
    return jnp.dot(a, b, preferred_element_type=F32)


def _dot_nt(a, b):
    return lax.dot_general(a, b, (((1,), (1,)), ((), ())), preferred_element_type=F32)


def _norm_matmul_kernel(x_ref, g_ref, w_ref, o_ref, h_ref):
    @pl.when(pl.program_id(1) == 0)
    def _():
        h_ref[...] = _rms(x_ref[...], g_ref[...]).astype(h_ref.dtype)

    o_ref[...] = _dot(h_ref[...], w_ref[...]).astype(o_ref.dtype)


def _norm_matmul(x, g, w, out_dtype, tm, tn):
    n, d = x.shape
    m = w.shape[1]
    tm = min(tm, n)
    return pl.pallas_call(
        _norm_matmul_kernel,
        grid=(n // tm, m // tn),
        in_specs=[
            pl.BlockSpec((tm, d), lambda i, j: (i, 0)),
            pl.BlockSpec((1, d), lambda i, j: (0, 0)),
            pl.BlockSpec((d, tn), lambda i, j: (0, j)),
        ],
        out_specs=pl.BlockSpec((tm, tn), lambda i, j: (i, j)),
        out_shape=jax.ShapeDtypeStruct((n, m), out_dtype),
        scratch_shapes=[pltpu.VMEM((tm, d), BF16)],
        compiler_params=_cparams("parallel", "arbitrary"),
    )(x, g.reshape(1, d), w)


def _matmul_norm_res_kernel(*refs, gated):
    if gated:
        a_ref, z_ref, ng_ref, w_ref, x_ref, g_ref, o_ref = refs
        ng = ng_ref[...]
        parts = []
        for h in range(a_ref.shape[1] // GDN_DV):
            sl = slice(h * GDN_DV, (h + 1) * GDN_DV)
            parts.append((_rms(a_ref[:, sl], ng) * _silu(z_ref[:, sl].astype(F32))).astype(BF16))
        a = jnp.concatenate(parts, axis=1)
    else:
        a_ref, w_ref, x_ref, g_ref, o_ref = refs
        a = a_ref[...].astype(BF16)
    o_ref[...] = x_ref[...] + _rms(_dot(a, w_ref[...]), g_ref[...])


def _matmul_norm_res(a, w, x, g, tm, gate=None):
    n, kdim = a.shape
    d = w.shape[1]
    in_specs = [pl.BlockSpec((tm, kdim), lambda i: (i, 0))]
    args = [a]
    if gate is not None:
        zsrc, zcol0, norm_g = gate
        zblk = zcol0 // kdim
        in_specs += [pl.BlockSpec((tm, kdim), lambda i: (i, zblk)),
                     pl.BlockSpec((1, GDN_DV), lambda i: (0, 0))]
        args += [zsrc, norm_g.reshape(1, GDN_DV)]
    in_specs += [pl.BlockSpec((kdim, d), lambda i: (0, 0), pipeline_mode=pl.Buffered(1)),
                 pl.BlockSpec((tm, d), lambda i: (i, 0)),
                 pl.BlockSpec((1, d), lambda i: (0, 0))]
    args += [w, x, g.reshape(1, d)]
    return pl.pallas_call(
        functools.partial(_matmul_norm_res_kernel, gated=gate is not None),
        grid=(n // tm,),
        in_specs=in_specs,
        out_specs=pl.BlockSpec((tm, d), lambda i: (i, 0)),
        out_shape=jax.ShapeDtypeStruct((n, d), F32),
        compiler_params=_cparams("parallel"),
    )(*args)


def _ffn_kernel(x_ref, g1_ref, wg_ref, wu_ref, wd_ref, g2_ref, o_ref, h_ref, acc_ref):
    f = pl.program_id(1)

    @pl.when(f == 0)
    def _():
        h_ref[...] = _rms(x_ref[...], g1_ref[...]).astype(BF16)
        acc_ref[...] = jnp.zeros(acc_ref.shape, F32)

    h = h_ref[...]
    gate = _dot(h, wg_ref[...])
    up = _dot(h, wu_ref[...])
    acc_ref[...] += _dot((_silu(gate) * up).astype(BF16), wd_ref[...])

    @pl.when(f == pl.num_programs(1) - 1)
    def _():
        o_ref[...] = x_ref[...] + _rms(acc_ref[...], g2_ref[...])


def _ffn(x, g1, w_gu, w_d, g2, tm, tf):
    n, d = x.shape
    fdim = w_d.shape[0]
    nf = fdim // tf
    tm = min(tm, n)
    return pl.pallas_call(
        _ffn_kernel,
        grid=(n // tm, nf),
        in_specs=[
            pl.BlockSpec((tm, d), lambda i, f: (i, 0)),
            pl.BlockSpec((1, d), lambda i, f: (0, 0)),
            pl.BlockSpec((d, tf), lambda i, f: (0, f)),
            pl.BlockSpec((d, tf), lambda i, f: (0, nf + f)),
            pl.BlockSpec((tf, d), lambda i, f: (f, 0)),
            pl.BlockSpec((1, d), lambda i, f: (0, 0)),
        ],
        out_specs=pl.BlockSpec((tm, d), lambda i, f: (i, 0)),
        out_shape=jax.ShapeDtypeStruct((n, d), F32),
        scratch_shapes=[pltpu.VMEM((tm, d), BF16), pltpu.VMEM((tm, d), F32)],
        compiler_params=_cparams("parallel", "arbitrary"),
    )(x, g1.reshape(1, d), w_gu, w_gu, w_d, g2.reshape(1, d))


def _gdn_gate_kernel(ba_ref, nega_ref, dtb_ref, o_ref, *, chunk):
    rows = ba_ref.shape[0]
    lane = lax.broadcasted_iota(jnp.int32, (chunk, GDN_GATE_DIM), 1)
    is_dt = (lane % (2 * GDN_HV)) >= GDN_HV
    is_bwd = lane >= 2 * GDN_HV
    ri = lax.broadcasted_iota(jnp.int32, (chunk, chunk), 0)
    ci = lax.broadcasted_iota(jnp.int32, (chunk, chunk), 1)
    tril = jnp.where(ri >= ci, 1.0, 0.0).astype(BF16)
    triu = jnp.where(ri <= ci, 1.0, 0.0).astype(BF16)
    for c in range(rows // chunk):
        x = ba_ref[c * chunk:(c + 1) * chunk, :]
        beta = 1.0 / (1.0 + jnp.exp(-x))
        y = x + dtb_ref[...]
        sp = jnp.maximum(y, 0.0) + jnp.log1p(jnp.exp(-jnp.abs(y)))
        g = jnp.where(is_dt, nega_ref[...] * sp, 0.0)
        hi = g.astype(BF16)
        lo = (g - hi.astype(F32)).astype(BF16)
        pre = _dot(tril, hi) + _dot(tril, lo)
        suf = _dot(triu, hi) + _dot(triu, lo)
        o_ref[c * chunk:(c + 1) * chunk, :] = jnp.where(is_dt, jnp.where(is_bwd, suf, pre), beta)


def _gdn_gates(ba, a_log, dt_bias, chunk, tt):
    n = ba.shape[0]
    zeros = jnp.zeros((2, GDN_HV), F32)
    nega = jnp.stack([zeros, -jnp.exp(a_log.astype(F32))], axis=1).reshape(1, GDN_GATE_DIM)
    dtb = jnp.stack([zeros, dt_bias.astype(F32)], axis=1).reshape(1, GDN_GATE_DIM)
    return pl.pallas_call(
        functools.partial(_gdn_gate_kernel, chunk=chunk),
        grid=(n // tt,),
        in_specs=[pl.BlockSpec((tt, GDN_GATE_DIM), lambda i: (i, 0)),
                  pl.BlockSpec((1, GDN_GATE_DIM), lambda i: (0, 0)),
                  pl.BlockSpec((1, GDN_GATE_DIM), lambda i: (0, 0))],
        out_specs=pl.BlockSpec((tt, GDN_GATE_DIM), lambda i: (i, 0)),
        out_shape=jax.ShapeDtypeStruct((n, GDN_GATE_DIM), F32),
        compiler_params=_cparams("parallel"),
    )(ba, nega, dtb)


def _gdn_prep_kernel(cur_ref, prev_ref, next_ref, cw_ref, o_ref, ext_ref):
    i = pl.program_id(1)
    j = pl.program_id(2)
    tt, tc = cur_ref.shape[1], cur_ref.shape[2]
    halo = prev_ref.shape[1]
    pad = GDN_CONV_W // 2
    ext_ref[0:halo, :] = jnp.where(i > 0, prev_ref[0].astype(F32), 0.0)
    ext_ref[halo:halo + tt, :] = cur_ref[0].astype(F32)
    ext_ref[halo + tt:2 * halo + tt, :] = jnp.where(i < pl.num_programs(1) - 1, next_ref[0].astype(F32), 0.0)
    acc = ext_ref[pl.ds(halo - pad, tt), :] * cw_ref[0:1, :]
    for w in range(1, GDN_CONV_W):
        acc = acc + ext_ref[pl.ds(halo - pad + w, tt), :] * cw_ref[w:w + 1, :]
    y = _silu(acc)
    col0 = j * tc
    is_qk = col0 < 2 * GDN_QK_DIM

    @pl.when(is_qk)
    def _():
        qscale = jnp.where(col0 < GDN_QK_DIM, GDN_DK ** -0.5, 1.0)
        for h in range(tc // GDN_DK):
            sl = slice(h * GDN_DK, (h + 1) * GDN_DK)
            yh = y[:, sl]
            inv = lax.rsqrt(jnp.sum(yh * yh, axis=-1, keepdims=True) + 1e-6)
            o_ref[0, :, sl] = (yh * inv * qscale).astype(o_ref.dtype)

    @pl.when(jnp.logical_not(is_qk))
    def _():
        o_ref[0] = y.astype(o_ref.dtype)


def _gdn_prep(proj, conv_w, tt, tc):
    b, t, _ = proj.shape
    halo = BF16_SUBLANE
    nh = t // halo
    r = tt // halo
    return pl.pallas_call(
        _gdn_prep_kernel,
        grid=(b, t // tt, GDN_CONV_DIM // tc),
        in_specs=[
            pl.BlockSpec((1, tt, tc), lambda bi, i, j: (bi, i, j)),
            pl.BlockSpec((1, halo, tc), lambda bi, i, j: (bi, jnp.maximum(i * r - 1, 0), j)),
            pl.BlockSpec((1, halo, tc), lambda bi, i, j: (bi, jnp.minimum((i + 1) * r, nh - 1), j)),
            pl.BlockSpec((GDN_CONV_W, tc), lambda bi, i, j: (0, j)),
        ],
        out_specs=pl.BlockSpec((1, tt, tc), lambda bi, i, j: (bi, i, j)),
        out_shape=jax.ShapeDtypeStruct((b, t, GDN_CONV_DIM), BF16),
        scratch_shapes=[pltpu.VMEM((tt + 2 * halo, tc), F32)],
        compiler_params=_cparams("parallel", "parallel", "parallel"),
    )(proj, proj, proj, conv_w)


def _tri_inverse_masks(c, upper):
    ri = lax.broadcasted_iota(jnp.int32, (c, c), 0)
    ci = lax.broadcasted_iota(jnp.int32, (c, c), 1)
    base_mask = (ri // GDN_INV_BASE) == (ci // GDN_INV_BASE)
    level_masks = []
    b = GDN_INV_BASE
    while b < c:
        mask = ((ri // (2 * b)) == (ci // (2 * b))) & ((ri // b) != (ci // b))
        level_masks.append(_take_blocks(mask, b, 0 if upper else 1))
        b *= 2
    return base_mask, level_masks


def _take_blocks(x, b, parity):
    return jnp.concatenate([x[s:s + b] for s in range(parity * b, x.shape[0], 2 * b)], axis=0)


def _merge_level(p, low, mask, b, upper):
    c = p.shape[0]
    par = 0 if upper else 1
    nblk = c // (2 * b)
    pb = p.astype(BF16)
    off = jnp.where(mask, _take_blocks(low, b, par), 0.0).astype(BF16)
    x = _dot(off, pb).astype(BF16)
    yield
    zero = jnp.zeros((b, c), BF16)
    pieces = []
    for i in range(nblk):
        xi = x[i * b:(i + 1) * b]
        pieces += [xi, zero] if upper else [zero, xi]
    y = _dot(_take_blocks(pb, b, par), jnp.concatenate(pieces, axis=0))
    yield
    rows = []
    for i in range(nblk):
        keep = p[(2 * i + 1 - par) * b:(2 * i + 2 - par) * b]
        upd = p[(2 * i + par) * b:(2 * i + par + 1) * b] - y[i * b:(i + 1) * b]
        rows += [upd, keep] if upper else [keep, upd]
    yield jnp.concatenate(rows, axis=0)


def _unit_tri_inverse(low, eye, masks, upper):
    c = low.shape[0]
    base_mask, level_masks = masks
    m0 = jnp.where(base_mask, -low, 0.0)
    p = eye + m0
    mb = m0.astype(BF16)
    m = _dot(mb, mb)
    yield
    steps = int(math.log2(GDN_INV_BASE))
    for s in range(2, steps + 1):
        mb = m.astype(BF16)
        if s < steps:
            r = _dot(mb, jnp.concatenate([mb, p.astype(BF16)], axis=1))
            m = r[:, :c]
            p = p + r[:, c:]
        else:
            p = p + _dot(mb, p.astype(BF16))
        yield
    b = GDN_INV_BASE
    for mask in level_masks:
        for out in _merge_level(p, low, mask, b, upper):
            if out is None:
                yield
        p = out
        b *= 2
    return p


def _delta_chain(qf, kf, kt, qk, kk, vf, s, beta, gcum, gcum_r, glast, eye, incl, strict, masks, upper):
    chunk = qf.shape[0]
    dec = jnp.where(incl, jnp.exp(gcum - gcum_r), 0.0)
    low = jnp.where(strict, kk * dec, 0.0) * beta
    tinv = yield from _unit_tri_inverse(low, eye, masks, upper)
    eg = jnp.exp(gcum)
    rhs = jnp.concatenate([vf * beta, kf * beta * eg], axis=1).astype(BF16)
    sol = _dot(tinv.astype(BF16), rhs)
    yield
    u, w = sol[:, :GDN_DV], sol[:, GDN_DV:]
    ws_qs = _dot(jnp.concatenate([w, qf * eg], axis=0).astype(BF16), s.astype(BF16))
    yield
    v_new = (u - ws_qs[:chunk]).astype(BF16)
    kend_t = kt * jnp.exp(glast - gcum_r)
    r = _dot(jnp.concatenate([(qk * dec).astype(BF16), kend_t.astype(BF16)], axis=0), v_new)
    yield
    yield ws_qs[chunk:] + r[:chunk], s * jnp.exp(glast) + r[chunk:]


def _run_interleaved(gens):
    results = [None] * len(gens)
    live = list(range(len(gens)))
    while live:
        nxt = []
        for i in live:
            try:
                results[i] = next(gens[i])
                nxt.append(i)
            except StopIteration:
                pass
        live = nxt
    return results


def _gdn_core_kernel(q_ref, k_ref, v_ref, gc_ref, gr_ref, o_ref, s_ref, *, group, chunk):
    t = q_ref.shape[1]
    nc = t // chunk
    ri = lax.broadcasted_iota(jnp.int32, (chunk, chunk), 0)
    ci = lax.broadcasted_iota(jnp.int32, (chunk, chunk), 1)
    eye = jnp.where(ri == ci, 1.0, 0.0).astype(F32)
    incl = (ri >= ci, ri <= ci)
    strict = (ri > ci, ri < ci)
    inv_masks = (_tri_inverse_masks(chunk, False), _tri_inverse_masks(chunk, True))
    s_ref[...] = jnp.zeros(s_ref.shape, F32)
    o_ref[...] = jnp.zeros(o_ref.shape, F32)

    def body(c, carry):
        chains = []
        rows = []
        for d in range(2):
            cc = c if d == 0 else nc - 1 - c
            rows.append(pl.ds(pl.multiple_of(cc * chunk, chunk), chunk))
            gcol = gc_ref[0, 0, rows[d], :]
            grow = gr_ref[0, 0, :, rows[d]]
            edge = chunk - 1 if d == 0 else 0
            for kh in range(group // 2):
                ksl = slice(kh * GDN_DK, (kh + 1) * GDN_DK)
                qb = q_ref[0, rows[d], ksl]
                kb = k_ref[0, rows[d], ksl]
                qf = qb.astype(F32)
                kf = kb.astype(F32)
                gram = _dot_nt(jnp.concatenate([qb, kb], axis=0), kb)
                kt = kf.T
                for j in (2 * kh, 2 * kh + 1):
                    cb = d * 2 * group + j
                    cg = cb + group
                    vf = v_ref[0, rows[d], j * GDN_DV:(j + 1) * GDN_DV].astype(F32)
                    chains.append(_delta_chain(
                        qf, kf, kt, gram[:chunk], gram[chunk:], vf, s_ref[d * group + j],
                        gcol[:, cb:cb + 1], gcol[:, cg:cg + 1], grow[cg:cg + 1, :], gcol[edge:edge + 1, cg:cg + 1],
                        eye, incl[d], strict[d], inv_masks[d], d == 1))
        outs = _run_interleaved(chains)
        for idx, (o_rows, s_new) in enumerate(outs):
            d, j = divmod(idx, group)
            o_ref[0, rows[d], j * GDN_DV:(j + 1) * GDN_DV] += o_rows
            s_ref[d * group + j] = s_new
        return carry

    lax.fori_loop(0, nc, body, 0)


def _gdn_core(qkv, gcol, grow, group, chunk):
    b, t, _ = qkv.shape
    ng = GDN_HV // group
    qw = (group // 2) * GDN_DK
    vw = group * GDN_DV
    kblk0 = GDN_QK_DIM // qw
    vblk0 = 2 * GDN_QK_DIM // vw
    once = pl.Buffered(1)
    return pl.pallas_call(
        functools.partial(_gdn_core_kernel, group=group, chunk=chunk),
        grid=(b, ng),
        in_specs=[
            pl.BlockSpec((1, t, qw), lambda bi, hg: (bi, 0, hg), pipeline_mode=once),
            pl.BlockSpec((1, t, qw), lambda bi, hg: (bi, 0, kblk0 + hg), pipeline_mode=once),
            pl.BlockSpec((1, t, vw), lambda bi, hg: (bi, 0, vblk0 + hg), pipeline_mode=once),
            pl.BlockSpec((1, 1, t, 4 * group), lambda bi, hg: (bi, hg, 0, 0)),
            pl.BlockSpec((1, 1, 4 * group, t), lambda bi, hg: (bi, hg, 0, 0)),
        ],
        out_specs=pl.BlockSpec((1, t, vw), lambda bi, hg: (bi, 0, hg), pipeline_mode=once),
        out_shape=jax.ShapeDtypeStruct((b, t, GDN_V_DIM), F32),
        scratch_shapes=[pltpu.VMEM((2 * group, GDN_DK, GDN_DV), F32)],
        compiler_params=_cparams("parallel", "parallel"),
    )(qkv, qkv, qkv, gcol, grow)


def _gated_deltanet_layer(x, p, j, li):
    b, t, d = x.shape
    n = b * t
    x2 = x.reshape(n, d)
    g_pre = p['pre_mix_g'][li]
    proj = _norm_matmul(x2, g_pre, p['gdn_w_main'][j], BF16, 1024, 2048)
    ba = _norm_matmul(x2, g_pre, p['gdn_w_gate'][j], F32, 1024, GDN_GATE_DIM)
    gates = _gdn_gates(ba, p['gdn_a_log'][j], p['gdn_dt_bias'][j], GDN_CHUNK, 512)
    grp = GDN_GROUP
    ng = GDN_HV // grp
    gcol = gates.reshape(b, t, 4, ng, grp).transpose(0, 3, 1, 2, 4).reshape(b, ng, t, 4 * grp)
    grow = jnp.swapaxes(gcol, 2, 3)
    qkv = _gdn_prep(proj.reshape(b, t, GDN_MAIN_DIM), p['gdn_conv_w'][j], 512, 512)
    o = _gdn_core(qkv, gcol, grow, grp, GDN_CHUNK)
    y = _matmul_norm_res(o.reshape(n, GDN_V_DIM), p['gdn_w_out'][j], x2, p['post_mix_g'][li], 256,
                         gate=(proj, GDN_CONV_DIM, p['gdn_norm_g'][j]))
    return y.reshape(b, t, d)


def _rel_bucket_table():
    qi = np.arange(SWA_BLOCK)[:, None]
    kj = np.arange(3 * SWA_BLOCK)[None, :]
    rel = kj - SWA_BLOCK - qi
    half = REL_BUCKETS // 2
    max_exact = half // 2
    nabs = np.abs(rel)
    large = max_exact + (np.log(np.maximum(nabs, 1).astype(np.float32) / max_exact)
                         / math.log(REL_MAX_DIST / max_exact) * (half - max_exact)).astype(np.int32)
    large = np.minimum(large, half - 1)
    bucket = np.where(rel > 0, half, 0) + np.where(nabs < max_exact, nabs, large)
    return np.where(nabs <= WINDOW, bucket, -1).astype(np.int32)


def _swa_kernel(q_ref, kp_ref, kc_ref, kn_ref, vp_ref, vc_ref, vn_ref, bucket_ref, relb_ref, sink_ref,
                o_ref, bias_ref):
    b, i = pl.program_id(0), pl.program_id(1)
    nb = pl.num_programs(1)
    blk = SWA_BLOCK

    @pl.when((b == 0) & (i == 0))
    def _():
        bucket = bucket_ref[...]
        for hq in range(SWA_HQ):
            acc = jnp.full(bucket.shape, NEG_BIG, F32)
            for bk in range(REL_BUCKETS):
                acc = jnp.where(bucket == bk, relb_ref[bk, hq], acc)
            bias_ref[hq * blk:(hq + 1) * blk, :] = acc

    rows = SWA_GROUP * blk
    col = lax.broadcasted_iota(jnp.int32, (1, 3 * blk), 1)
    outside = ((col < blk) & (i == 0)) | ((col >= 2 * blk) & (i == nb - 1))
    edge = jnp.where(outside, NEG_BIG, 0.0)
    for h in range(SWA_HKV):
        hsl = slice(h * SWA_DH, (h + 1) * SWA_DH)
        qs = jnp.concatenate(
            [q_ref[0, :, (h * SWA_GROUP + g) * SWA_DH:(h * SWA_GROUP + g + 1) * SWA_DH] for g in range(SWA_GROUP)],
            axis=0)
        kcat = jnp.concatenate([kp_ref[0, :, hsl], kc_ref[0, :, hsl], kn_ref[0, :, hsl]], axis=0)
        vcat = jnp.concatenate([vp_ref[0, :, hsl], vc_ref[0, :, hsl], vn_ref[0, :, hsl]], axis=0)
        s = _dot_nt(qs, kcat) * SWA_DH ** -0.5 + (bias_ref[h * rows:(h + 1) * rows, :] + edge)
        sink = jnp.concatenate(
            [jnp.full((blk, 1), sink_ref[h * SWA_GROUP + g], F32) for g in range(SWA_GROUP)], axis=0)
        m = jnp.maximum(jnp.max(s, axis=-1, keepdims=True), sink)
        pexp = jnp.exp(s - m)
        denom = jnp.sum(pexp, axis=-1, keepdims=True) + jnp.exp(sink - m)
        o = _dot(pexp.astype(BF16), vcat) / denom
        for g in range(SWA_GROUP):
            osl = slice((h * SWA_GROUP + g) * SWA_DH, (h * SWA_GROUP + g + 1) * SWA_DH)
            o_ref[0, :, osl] = o[g * blk:(g + 1) * blk].astype(o_ref.dtype)


def _swa_attention(qkv, rel_bias, sink):
    b, t, _ = qkv.shape
    nb = t // SWA_BLOCK
    qw = SWA_HQ * SWA_DH
    kvw = SWA_HKV * SWA_DH
    kblk = qw // kvw
    kv_spec = lambda off, cblk: pl.BlockSpec(
        (1, SWA_BLOCK, kvw), lambda bi, i: (bi, jnp.clip(i + off, 0, nb - 1), cblk))
    return pl.pallas_call(
        _swa_kernel,
        grid=(b, nb),
        in_specs=[
            pl.BlockSpec((1, SWA_BLOCK, qw), lambda bi, i: (bi, i, 0)),
            kv_spec(-1, kblk), kv_spec(0, kblk), kv_spec(1, kblk),
            kv_spec(-1, kblk + 1), kv_spec(0, kblk + 1), kv_spec(1, kblk + 1),
            pl.BlockSpec((SWA_BLOCK, 3 * SWA_BLOCK), lambda bi, i: (0, 0)),
            pl.BlockSpec(memory_space=pltpu.SMEM),
            pl.BlockSpec(memory_space=pltpu.SMEM),
        ],
        out_specs=pl.BlockSpec((1, SWA_BLOCK, qw), lambda bi, i: (bi, i, 0)),
        out_shape=jax.ShapeDtypeStruct((b, t, qw), BF16),
        scratch_shapes=[pltpu.VMEM((SWA_HQ * SWA_BLOCK, 3 * SWA_BLOCK), F32)],
        compiler_params=_cparams("arbitrary", "arbitrary"),
    )(qkv, qkv, qkv, qkv, qkv, qkv, qkv, jnp.asarray(_rel_bucket_table()), rel_bias.astype(F32),
      sink.astype(F32))


def _swa_layer(x, p, j, li):
    b, t, d = x.shape
    n = b * t
    x2 = x.reshape(n, d)
    qkv = _norm_matmul(x2, p['pre_mix_g'][li], p['swa_w_in'][j], BF16, 1024, 1024)
    o = _swa_attention(qkv.reshape(b, t, -1), p['rel_bias'], p['swa_sink'][j])
    y = _matmul_norm_res(o.reshape(n, d), p['swa_w_out'][j], x2, p['post_mix_g'][li], 512)
    return y.reshape(b, t, d)


def _fnet_chan_kernel(x_ref, g_ref, w_ref, o_ref, h_ref):
    gi = pl.program_id(1)

    @pl.when(gi == 0)
    def _():
        h_ref[...] = _rms(x_ref[...], g_ref[...]).astype(BF16)

    hg = h_ref[:, pl.ds(pl.multiple_of(gi * FNET_GW, FNET_GW), FNET_GW)]
    r = _dot(hg, w_ref[...])
    o_ref[0, 0] = r[:, :FNET_GW].astype(o_ref.dtype)
    o_ref[0, 1] = r[:, FNET_GW:].astype(o_ref.dtype)


def _fnet_chan(x, g, w_cs, tm):
    b, t, d = x.shape
    tm = min(tm, t)
    nt = t // tm
    return pl.pallas_call(
        _fnet_chan_kernel,
        grid=(b * nt, FNET_GROUPS),
        in_specs=[
            pl.BlockSpec((tm, d), lambda i, gi: (i, 0)),
            pl.BlockSpec((1, d), lambda i, gi: (0, 0)),
            pl.BlockSpec((FNET_GW, 2 * FNET_GW), lambda i, gi: (0, 0)),
        ],
        out_specs=pl.BlockSpec((1, 2, tm, FNET_GW), lambda i, gi: (i // nt, 0, i % nt, gi)),
        out_shape=jax.ShapeDtypeStruct((b, 2, t, d), BF16),
        scratch_shapes=[pltpu.VMEM((tm, d), BF16)],
        compiler_params=_cparams("parallel", "arbitrary"),
    )(x.reshape(b * t, d), g.reshape(1, d), w_cs)


def _bmm_kernel(a_ref, b_ref, o_ref, acc_ref):
    k = pl.program_id(2)

    @pl.when(k == 0)
    def _():
        acc_ref[...] = jnp.zeros(acc_ref.shape, F32)

    acc_ref[...] += _dot(a_ref[...], b_ref[0])

    @pl.when(k == pl.num_programs(2) - 1)
    def _():
        o_ref[0] = acc_ref[...].astype(o_ref.dtype)


def _shared_lhs_bmm(a, bm, tm, tk):
    m, kdim = a.shape
    b, _, d = bm.shape
    tm = min(tm, m)
    return pl.pallas_call(
        _bmm_kernel,
        grid=(b, m // tm, kdim // tk),
        in_specs=[
            pl.BlockSpec((tm, tk), lambda bi, i, k: (i, k)),
            pl.BlockSpec((1, tk, d), lambda bi, i, k: (bi, k, 0)),
        ],
        out_specs=pl.BlockSpec((1, tm, d), lambda bi, i, k: (bi, i, 0)),
        out_shape=jax.ShapeDtypeStruct((b, m, d), BF16),
        scratch_shapes=[pltpu.VMEM((tm, d), F32)],
        compiler_params=_cparams("parallel", "parallel", "arbitrary"),
    )(a, bm)


def _dft_tables(t):
    def angle(n):
        idx = jnp.arange(n, dtype=jnp.int32)
        return (idx[:, None] * idx[None, :] % n).astype(F32) * (2.0 * math.pi / n)

    ac = angle(FNET_GW)
    at = angle(t)
    scale = 1.0 / math.sqrt(t * FNET_GW)
    w_cs = jnp.concatenate([jnp.cos(ac), jnp.sin(ac)], axis=1).astype(BF16)
    f_t = (jnp.concatenate([jnp.cos(at), -jnp.sin(at)], axis=1) * scale).astype(BF16)
    return w_cs, f_t


def _fnet_layer(x, p, j, li):
    b, t, d = x.shape
    n = b * t
    w_cs, f_t = _dft_tables(t)
    pcs = _fnet_chan(x, p['pre_mix_g'][li], w_cs, 1024)
    y = _shared_lhs_bmm(f_t, pcs.reshape(b, 2 * t, d), 1024, 1024)
    out = _matmul_norm_res(y.reshape(n, d), p['fnet_w_out'][j], x.reshape(n, d), p['post_mix_g'][li], 512)
    return out.reshape(b, t, d)


def _trunk(x, p):
    b, t, d = x.shape
    for i in range(DEPTH):
        kind, j = i % N_MIXERS, i // N_MIXERS
        if kind == 0:
            x = _gated_deltanet_layer(x, p, j, i)
        elif kind == 1:
            x = _swa_layer(x, p, j, i)
        else:
            x = _fnet_layer(x, p, j, i)
        x = _ffn(x.reshape(b * t, d), p['pre_ffn_g'][i], p['ffn_w_gate_up'][i], p['ffn_w_down'][i],
                 p['post_ffn_g'][i], 512, 512).reshape(b, t, d)
    return x


def kernel(x_prompt, x_sample, pre_mix_g, post_mix_g, pre_ffn_g, post_ffn_g, gdn_w_in, gdn_conv_w, gdn_a_log,
           gdn_dt_bias, gdn_norm_g, gdn_w_out, swa_w_in, swa_sink, swa_w_out, rel_bias, fnet_w_out,
           ffn_w_gate_up, ffn_w_down):
    p = dict(
        pre_mix_g=pre_mix_g, post_mix_g=post_mix_g, pre_ffn_g=pre_ffn_g, post_ffn_g=post_ffn_g,
        gdn_w_main=gdn_w_in[:, :, :GDN_MAIN_DIM].astype(BF16),
        gdn_w_gate=gdn_w_in[:, :, GDN_MAIN_DIM:].astype(BF16),
        gdn_conv_w=gdn_conv_w, gdn_a_log=gdn_a_log, gdn_dt_bias=gdn_dt_bias, gdn_norm_g=gdn_norm_g,
        gdn_w_out=gdn_w_out.astype(BF16),
        swa_w_in=swa_w_in.astype(BF16), swa_sink=swa_sink, swa_w_out=swa_w_out.astype(BF16),
        rel_bias=rel_bias, fnet_w_out=fnet_w_out.astype(BF16),
        ffn_w_gate_up=ffn_w_gate_up.astype(BF16), ffn_w_down=ffn_w_down.astype(BF16),
    )
    return _trunk(x_prompt, p), _trunk(x_sample, p)
```

```python
import functools
import math

import numpy as np
import jax
import jax.numpy as jnp
from jax import lax
from jax.experimental import pallas as pl
from jax.experimental.pallas import tpu as pltpu

D_MODEL = 2048
DEPTH = 4
N_MIXERS = 3

GDN_DK = 128
GDN_DV = 128
GDN_HK = D_MODEL // 128
GDN_HV = 2 * GDN_HK
GDN_QK_DIM = GDN_HK * GDN_DK
GDN_V_DIM = GDN_HV * GDN_DV
GDN_CONV_DIM = 2 * GDN_QK_DIM + GDN_V_DIM
GDN_CONV_W = 5
GDN_MAIN_DIM = GDN_CONV_DIM + GDN_V_DIM
GDN_GATE_DIM = 4 * GDN_HV
GDN_CHUNK = 128
GDN_GROUP = 8
GDN_INV_BASE = 16

SWA_DH = 128
SWA_HQ = D_MODEL // SWA_DH
SWA_HKV = 4
SWA_GROUP = SWA_HQ // SWA_HKV
WINDOW = 128
SWA_BLOCK = 128
REL_BUCKETS = 32
REL_MAX_DIST = 128

FNET_GROUPS = 4
FNET_GW = D_MODEL // FNET_GROUPS

FFN_HIDDEN = -(-8 * D_MODEL // (3 * 256)) * 256

RMS_EPS = 1e-6
NEG_BIG = -1e30

LANE = 128
BF16_SUBLANE = 16
VMEM_LIMIT_BYTES = 56 * 1024 * 1024

BF16 = jnp.bfloat16
F32 = jnp.float32


def _cparams(*sem):
    return pltpu.CompilerParams(dimension_semantics=sem, vmem_limit_bytes=VMEM_LIMIT_BYTES)


def _rms(x, g):
    return x * lax.rsqrt(jnp.mean(x * x, axis=-1, keepdims=True) + RMS_EPS) * g


def _silu(x):
    return x * (1.0 / (1.0 + jnp.exp(-x)))


def _dot(a, b):
    return jnp.dot(a, b, preferred_element_type=F32)


def _dot_nt(a, b):
    return lax.dot_general(a, b, (((1,), (1,)), ((), ())), preferred_element_type=F32)


def _norm_matmul_kernel(x_ref, g_ref, w_ref, o_ref, h_ref):
    @pl.when(pl.program_id(1) == 0)
    def _():
        h_ref[...] = _rms(x_ref[...], g_ref[...]).astype(h_ref.dtype)

    o_ref[...] = _dot(h_ref[...], w_ref[...]).astype(o_ref.dtype)


def _norm_matmul(x, g, w, out_dtype, tm, tn):
    n, d = x.shape
    m = w.shape[1]
    tm = min(tm, n)
    return pl.pallas_call(
        _norm_matmul_kernel,
        grid=(n // tm, m // tn),
        in_specs=[
            pl.BlockSpec((tm, d), lambda i, j: (i, 0)),
            pl.BlockSpec((1, d), lambda i, j: (0, 0)),
            pl.BlockSpec((d, tn), lambda i, j: (0, j)),
        ],
        out_specs=pl.BlockSpec((tm, tn), lambda i, j: (i, j)),
        out_shape=jax.ShapeDtypeStruct((n, m), out_dtype),
        scratch_shapes=[pltpu.VMEM((tm, d), BF16)],
        compiler_params=_cparams("parallel", "arbitrary"),
    )(x, g.reshape(1, d), w)


def _matmul_norm_res_kernel(*refs, gated):
    if gated:
        a_ref, z_ref, ng_ref, w_ref, x_ref, g_ref, o_ref = refs
        ng = ng_ref[...]
        parts = []
        for h in range(a_ref.shape[1] // GDN_DV):
            sl = slice(h * GDN_DV, (h + 1) * GDN_DV)
            parts.append((_rms(a_ref[:, sl], ng) * _silu(z_ref[:, sl].astype(F32))).astype(BF16))
        a = jnp.concatenate(parts, axis=1)
    else:
        a_ref, w_ref, x_ref, g_ref, o_ref = refs
        a = a_ref[...].astype(BF16)
    o_ref[...] = x_ref[...] + _rms(_dot(a, w_ref[...]), g_ref[...])


def _matmul_norm_res(a, w, x, g, tm, gate=None):
    n, kdim = a.shape
    d = w.shape[1]
    in_specs = [pl.BlockSpec((tm, kdim), lambda i: (i, 0))]
    args = [a]
    if gate is not None:
        zsrc, zcol0, norm_g = gate
        zblk = zcol0 // kdim
        in_specs += [pl.BlockSpec((tm, kdim), lambda i: (i, zblk)),
                     pl.BlockSpec((1, GDN_DV), lambda i: (0, 0))]
        args += [zsrc, norm_g.reshape(1, GDN_DV)]
    in_specs += [pl.BlockSpec((kdim, d), lambda i: (0, 0), pipeline_mode=pl.Buffered(1)),
                 pl.BlockSpec((tm, d), lambda i: (i, 0)),
                 pl.BlockSpec((1, d), lambda i: (0, 0))]
    args += [w, x, g.reshape(1, d)]
    return pl.pallas_call(
        functools.partial(_matmul_norm_res_kernel, gated=gate is not None),
        grid=(n // tm,),
        in_specs=in_specs,
        out_specs=pl.BlockSpec((tm, d), lambda i: (i, 0)),
        out_shape=jax.ShapeDtypeStruct((n, d), F32),
        compiler_params=_cparams("parallel"),
    )(*args)


def _ffn_kernel(x_ref, g1_ref, wg_ref, wu_ref, wd_ref, g2_ref, o_ref, h_ref, acc_ref):
    f = pl.program_id(1)

    @pl.when(f == 0)
    def _():
        h_ref[...] = _rms(x_ref[...], g1_ref[...]).astype(BF16)
        acc_ref[...] = jnp.zeros(acc_ref.shape, F32)

    h = h_ref[...]
    gate = _dot(h, wg_ref[...])
    up = _dot(h, wu_ref[...])
    acc_ref[...] += _dot((_silu(gate) * up).astype(BF16), wd_ref[...])

    @pl.when(f == pl.num_programs(1) - 1)
    def _():
        o_ref[...] = x_ref[...] + _rms(acc_ref[...], g2_ref[...])


def _ffn(x, g1, w_gu, w_d, g2, tm, tf):
    n, d = x.shape
    fdim = w_d.shape[0]
    nf = fdim // tf
    tm = min(tm, n)
    return pl.pallas_call(
        _ffn_kernel,
        grid=(n // tm, nf),
        in_specs=[
            pl.BlockSpec((tm, d), lambda i, f: (i, 0)),
            pl.BlockSpec((1, d), lambda i, f: (0, 0)),
            pl.BlockSpec((d, tf), lambda i, f: (0, f)),
            pl.BlockSpec((d, tf), lambda i, f: (0, nf + f)),
            pl.BlockSpec((tf, d), lambda i, f: (f, 0)),
            pl.BlockSpec((1, d), lambda i, f: (0, 0)),
        ],
        out_specs=pl.BlockSpec((tm, d), lambda i, f: (i, 0)),
        out_shape=jax.ShapeDtypeStruct((n, d), F32),
        scratch_shapes=[pltpu.VMEM((tm, d), BF16), pltpu.VMEM((tm, d), F32)],
        compiler_params=_cparams("parallel", "arbitrary"),
    )(x, g1.reshape(1, d), w_gu, w_gu, w_d, g2.reshape(1, d))


def _gdn_gate_kernel(ba_ref, nega_ref, dtb_ref, o_ref, *, chunk):
    rows = ba_ref.shape[0]
    lane = lax.broadcasted_iota(jnp.int32, (chunk, GDN_GATE_DIM), 1)
    is_dt = (lane % (2 * GDN_HV)) >= GDN_HV
    is_bwd = lane >= 2 * GDN_HV
    ri = lax.broadcasted_iota(jnp.int32, (chunk, chunk), 0)
    ci = lax.broadcasted_iota(jnp.int32, (chunk, chunk), 1)
    tril = jnp.where(ri >= ci, 1.0, 0.0).astype(BF16)
    triu = jnp.where(ri <= ci, 1.0, 0.0).astype(BF16)
    for c in range(rows // chunk):
        x = ba_ref[c * chunk:(c + 1) * chunk, :]
        beta = 1.0 / (1.0 + jnp.exp(-x))
        y = x + dtb_ref[...]
        sp = jnp.maximum(y, 0.0) + jnp.log1p(jnp.exp(-jnp.abs(y)))
        g = jnp.where(is_dt, nega_ref[...] * sp, 0.0)
        hi = g.astype(BF16)
        lo = (g - hi.astype(F32)).astype(BF16)
        pre = _dot(tril, hi) + _dot(tril, lo)
        suf = _dot(triu, hi) + _dot(triu, lo)
        o_ref[c * chunk:(c + 1) * chunk, :] = jnp.where(is_dt, jnp.where(is_bwd, suf, pre), beta)


def _gdn_gates(ba, a_log, dt_bias, chunk, tt):
    n = ba.shape[0]
    zeros = jnp.zeros((2, GDN_HV), F32)
    nega = jnp.stack([zeros, -jnp.exp(a_log.astype(F32))], axis=1).reshape(1, GDN_GATE_DIM)
    dtb = jnp.stack([zeros, dt_bias.astype(F32)], axis=1).reshape(1, GDN_GATE_DIM)
    return pl.pallas_call(
        functools.partial(_gdn_gate_kernel, chunk=chunk),
        grid=(n // tt,),
        in_specs=[pl.BlockSpec((tt, GDN_GATE_DIM), lambda i: (i, 0)),
                  pl.BlockSpec((1, GDN_GATE_DIM), lambda i: (0, 0)),
                  pl.BlockSpec((1, GDN_GATE_DIM), lambda i: (0, 0))],
        out_specs=pl.BlockSpec((tt, GDN_GATE_DIM), lambda i: (i, 0)),
        out_shape=jax.ShapeDtypeStruct((n, GDN_GATE_DIM), F32),
        compiler_params=_cparams("parallel"),
    )(ba, nega, dtb)


GDN_PREP_ROWS = 128
GDN_CONV_SIDE_TAPS = tuple(w for w in range(GDN_CONV_W) if w != GDN_CONV_W // 2)


def _conv_shift_matrix(halo):
    r = GDN_PREP_ROWS
    pad = GDN_CONV_W // 2
    sh = np.zeros((len(GDN_CONV_SIDE_TAPS) * r, r + 2 * halo), np.float32)
    for k, w in enumerate(GDN_CONV_SIDE_TAPS):
        sh[k * r + np.arange(r), halo + np.arange(r) + w - pad] = 1.0
    return sh


def _gdn_prep_kernel(cur_ref, prev_ref, next_ref, cw_ref, sh_ref, o_ref, ext_ref):
    i = pl.program_id(1)
    j = pl.program_id(2)
    tt, tc = cur_ref.shape[1], cur_ref.shape[2]
    halo = prev_ref.shape[1]
    pad = GDN_CONV_W // 2
    r = GDN_PREP_ROWS
    zero = jnp.zeros((halo, tc), BF16)
    ext_ref[0:halo, :] = jnp.where(i > 0, prev_ref[0], zero)
    ext_ref[halo:halo + tt, :] = cur_ref[0]
    ext_ref[halo + tt:2 * halo + tt, :] = jnp.where(i < pl.num_programs(1) - 1, next_ref[0], zero)
    col0 = j * tc

    def conv_silu(rb):
        window = ext_ref[rb * r:rb * r + r + 2 * halo, :]
        side = _dot(sh_ref[...], window)
        acc = window[halo:halo + r].astype(F32) * cw_ref[pad:pad + 1, :]
        for k, w in enumerate(GDN_CONV_SIDE_TAPS):
            acc = acc + side[k * r:(k + 1) * r] * cw_ref[w:w + 1, :]
        return _silu(acc)

    @pl.when(col0 < 2 * GDN_QK_DIM)
    def _():
        qscale = jnp.where(col0 < GDN_QK_DIM, GDN_DK ** -0.5, 1.0)
        for rb in range(tt // r):
            y = conv_silu(rb)
            for h in range(tc // GDN_DK):
                sl = slice(h * GDN_DK, (h + 1) * GDN_DK)
                yh = y[:, sl]
                inv = lax.rsqrt(jnp.sum(yh * yh, axis=-1, keepdims=True) + 1e-6)
                o_ref[0, rb * r:(rb + 1) * r, sl] = (yh * inv * qscale).astype(o_ref.dtype)

    @pl.when(col0 >= 2 * GDN_QK_DIM)
    def _():
        for rb in range(tt // r):
            o_ref[0, rb * r:(rb + 1) * r, :] = conv_silu(rb).astype(o_ref.dtype)


def _gdn_prep(proj, conv_w, tt, tc):
    b, t, _ = proj.shape
    halo = BF16_SUBLANE
    nh = t // halo
    r = tt // halo
    sh = jnp.asarray(_conv_shift_matrix(halo), BF16)
    return pl.pallas_call(
        _gdn_prep_kernel,
        grid=(b, t // tt, GDN_CONV_DIM // tc),
        in_specs=[
            pl.BlockSpec((1, tt, tc), lambda bi, i, j: (bi, i, j)),
            pl.BlockSpec((1, halo, tc), lambda bi, i, j: (bi, jnp.maximum(i * r - 1, 0), j)),
            pl.BlockSpec((1, halo, tc), lambda bi, i, j: (bi, jnp.minimum((i + 1) * r, nh - 1), j)),
            pl.BlockSpec((GDN_CONV_W, tc), lambda bi, i, j: (0, j)),
            pl.BlockSpec(sh.shape, lambda bi, i, j: (0, 0)),
        ],
        out_specs=pl.BlockSpec((1, tt, tc), lambda bi, i, j: (bi, i, j)),
        out_shape=jax.ShapeDtypeStruct((b, t, GDN_CONV_DIM), BF16),
        scratch_shapes=[pltpu.VMEM((tt + 2 * halo, tc), BF16)],
        compiler_params=_cparams("parallel", "parallel", "parallel"),
    )(proj, proj, proj, conv_w, sh)


def _tri_inverse_masks(c, upper):
    ri = lax.broadcasted_iota(jnp.int32, (c, c), 0)
    ci = lax.broadcasted_iota(jnp.int32, (c, c), 1)
    base_mask = (ri // GDN_INV_BASE) == (ci // GDN_INV_BASE)
    level_masks = []
    b = GDN_INV_BASE
    while b < c:
        mask = ((ri // (2 * b)) == (ci // (2 * b))) & ((ri // b) != (ci // b))
        level_masks.append(_take_blocks(mask, b, 0 if upper else 1))
        b *= 2
    return base_mask, level_masks


def _take_blocks(x, b, parity):
    return jnp.concatenate([x[s:s + b] for s in range(parity * b, x.shape[0], 2 * b)], axis=0)


def _merge_level(p, low, mask, b, upper):
    c = p.shape[0]
    par = 0 if upper else 1
    nblk = c // (2 * b)
    pb = p.astype(BF16)
    off = jnp.where(mask, _take_blocks(low, b, par), 0.0).astype(BF16)
    x = _dot(off, pb).astype(BF16)
    yield
    zero = jnp.zeros((b, c), BF16)
    pieces = []
    for i in range(nblk):
        xi = x[i * b:(i + 1) * b]
        pieces += [xi, zero] if upper else [zero, xi]
    y = _dot(_take_blocks(pb, b, par), jnp.concatenate(pieces, axis=0))
    yield
    rows = []
    for i in range(nblk):
        keep = p[(2 * i + 1 - par) * b:(2 * i + 2 - par) * b]
        upd = p[(2 * i + par) * b:(2 * i + par + 1) * b] - y[i * b:(i + 1) * b]
        rows += [upd, keep] if upper else [keep, upd]
    yield jnp.concatenate(rows, axis=0)


def _unit_tri_inverse(low, eye, masks, upper):
    c = low.shape[0]
    base_mask, level_masks = masks
    m0 = jnp.where(base_mask, -low, 0.0)
    p = eye + m0
    mb = m0.astype(BF16)
    m = _dot(mb, mb)
    yield
    steps = int(math.log2(GDN_INV_BASE))
    for s in range(2, steps + 1):
        mb = m.astype(BF16)
        if s < steps:
            r = _dot(mb, jnp.concatenate([mb, p.astype(BF16)], axis=1))
            m = r[:, :c]
            p = p + r[:, c:]
        else:
            p = p + _dot(mb, p.astype(BF16))
        yield
    b = GDN_INV_BASE
    for mask in level_masks:
        for out in _merge_level(p, low, mask, b, upper):
            if out is None:
                yield
        p = out
        b *= 2
    return p


def _delta_chain(qf, kf, kt, qk, kk, vf, s, beta, gcum, gcum_r, glast, eye, incl, strict, masks, upper):
    chunk = qf.shape[0]
    dec = jnp.where(incl, jnp.exp(gcum - gcum_r), 0.0)
    low = jnp.where(strict, kk * dec, 0.0) * beta
    tinv = yield from _unit_tri_inverse(low, eye, masks, upper)
    eg = jnp.exp(gcum)
    rhs = jnp.concatenate([vf * beta, kf * beta * eg], axis=1).astype(BF16)
    sol = _dot(tinv.astype(BF16), rhs)
    yield
    u, w = sol[:, :GDN_DV], sol[:, GDN_DV:]
    ws_qs = _dot(jnp.concatenate([w, qf * eg], axis=0).astype(BF16), s.astype(BF16))
    yield
    v_new = (u - ws_qs[:chunk]).astype(BF16)
    kend_t = kt * jnp.exp(glast - gcum_r)
    r = _dot(jnp.concatenate([(qk * dec).astype(BF16), kend_t.astype(BF16)], axis=0), v_new)
    yield
    yield ws_qs[chunk:] + r[:chunk], s * jnp.exp(glast) + r[chunk:]


def _run_interleaved(gens):
    results = [None] * len(gens)
    live = list(range(len(gens)))
    while live:
        nxt = []
        for i in live:
            try:
                results[i] = next(gens[i])
                nxt.append(i)
            except StopIteration:
                pass
        live = nxt
    return results


def _gdn_core_kernel(q_ref, k_ref, v_ref, gc_ref, gr_ref, o_ref, s_ref, *, group, chunk):
    t = q_ref.shape[1]
    nc = t // chunk
    ri = lax.broadcasted_iota(jnp.int32, (chunk, chunk), 0)
    ci = lax.broadcasted_iota(jnp.int32, (chunk, chunk), 1)
    eye = jnp.where(ri == ci, 1.0, 0.0).astype(F32)
    incl = (ri >= ci, ri <= ci)
    strict = (ri > ci, ri < ci)
    inv_masks = (_tri_inverse_masks(chunk, False), _tri_inverse_masks(chunk, True))
    s_ref[...] = jnp.zeros(s_ref.shape, F32)
    o_ref[...] = jnp.zeros(o_ref.shape, F32)

    def body(c, carry):
        chains = []
        rows = []
        for d in range(2):
            cc = c if d == 0 else nc - 1 - c
            rows.append(pl.ds(pl.multiple_of(cc * chunk, chunk), chunk))
            gcol = gc_ref[0, 0, rows[d], :]
            grow = gr_ref[0, 0, :, rows[d]]
            edge = chunk - 1 if d == 0 else 0
            for kh in range(group // 2):
                ksl = slice(kh * GDN_DK, (kh + 1) * GDN_DK)
                qb = q_ref[0, rows[d], ksl]
                kb = k_ref[0, rows[d], ksl]
                qf = qb.astype(F32)
                kf = kb.astype(F32)
                gram = _dot_nt(jnp.concatenate([qb, kb], axis=0), kb)
                kt = kf.T
                for j in (2 * kh, 2 * kh + 1):
                    cb = d * 2 * group + j
                    cg = cb + group
                    vf = v_ref[0, rows[d], j * GDN_DV:(j + 1) * GDN_DV].astype(F32)
                    chains.append(_delta_chain(
                        qf, kf, kt, gram[:chunk], gram[chunk:], vf, s_ref[d * group + j],
                        gcol[:, cb:cb + 1], gcol[:, cg:cg + 1], grow[cg:cg + 1, :], gcol[edge:edge + 1, cg:cg + 1],
                        eye, incl[d], strict[d], inv_masks[d], d == 1))
        outs = _run_interleaved(chains)
        for idx, (o_rows, s_new) in enumerate(outs):
            d, j = divmod(idx, group)
            o_ref[0, rows[d], j * GDN_DV:(j + 1) * GDN_DV] += o_rows
            s_ref[d * group + j] = s_new
        return carry

    lax.fori_loop(0, nc, body, 0)


def _gdn_core(qkv, gcol, grow, group, chunk):
    b, t, _ = qkv.shape
    ng = GDN_HV // group
    qw = (group // 2) * GDN_DK
    vw = group * GDN_DV
    kblk0 = GDN_QK_DIM // qw
    vblk0 = 2 * GDN_QK_DIM // vw
    once = pl.Buffered(1)
    return pl.pallas_call(
        functools.partial(_gdn_core_kernel, group=group, chunk=chunk),
        grid=(b, ng),
        in_specs=[
            pl.BlockSpec((1, t, qw), lambda bi, hg: (bi, 0, hg), pipeline_mode=once),
            pl.BlockSpec((1, t, qw), lambda bi, hg: (bi, 0, kblk0 + hg), pipeline_mode=once),
            pl.BlockSpec((1, t, vw), lambda bi, hg: (bi, 0, vblk0 + hg), pipeline_mode=once),
            pl.BlockSpec((1, 1, t, 4 * group), lambda bi, hg: (bi, hg, 0, 0)),
            pl.BlockSpec((1, 1, 4 * group, t), lambda bi, hg: (bi, hg, 0, 0)),
        ],
        out_specs=pl.BlockSpec((1, t, vw), lambda bi, hg: (bi, 0, hg), pipeline_mode=once),
        out_shape=jax.ShapeDtypeStruct((b, t, GDN_V_DIM), F32),
        scratch_shapes=[pltpu.VMEM((2 * group, GDN_DK, GDN_DV), F32)],
        compiler_params=_cparams("parallel", "parallel"),
    )(qkv, qkv, qkv, gcol, grow)


def _gated_deltanet_layer(x, p, j, li):
    b, t, d = x.shape
    n = b * t
    x2 = x.reshape(n, d)
    g_pre = p['pre_mix_g'][li]
    proj = _norm_matmul(x2, g_pre, p['gdn_w_main'][j], BF16, 1024, 2048)
    ba = _norm_matmul(x2, g_pre, p['gdn_w_gate'][j], F32, 1024, GDN_GATE_DIM)
    gates = _gdn_gates(ba, p['gdn_a_log'][j], p['gdn_dt_bias'][j], GDN_CHUNK, 512)
    grp = GDN_GROUP
    ng = GDN_HV // grp
    gcol = gates.reshape(b, t, 4, ng, grp).transpose(0, 3, 1, 2, 4).reshape(b, ng, t, 4 * grp)
    grow = jnp.swapaxes(gcol, 2, 3)
    qkv = _gdn_prep(proj.reshape(b, t, GDN_MAIN_DIM), p['gdn_conv_w'][j], 512, 512)
    o = _gdn_core(qkv, gcol, grow, grp, GDN_CHUNK)
    y = _matmul_norm_res(o.reshape(n, GDN_V_DIM), p['gdn_w_out'][j], x2, p['post_mix_g'][li], 256,
                         gate=(proj, GDN_CONV_DIM, p['gdn_norm_g'][j]))
    return y.reshape(b, t, d)


def _rel_bucket_table():
    qi = np.arange(SWA_BLOCK)[:, None]
    kj = np.arange(3 * SWA_BLOCK)[None, :]
    rel = kj - SWA_BLOCK - qi
    half = REL_BUCKETS // 2
    max_exact = half // 2
    nabs = np.abs(rel)
    large = max_exact + (np.log(np.maximum(nabs, 1).astype(np.float32) / max_exact)
                         / math.log(REL_MAX_DIST / max_exact) * (half - max_exact)).astype(np.int32)
    large = np.minimum(large, half - 1)
    bucket = np.where(rel > 0, half, 0) + np.where(nabs < max_exact, nabs, large)
    return np.where(nabs <= WINDOW, bucket, -1).astype(np.int32)


SWA_STEP_BLOCKS = 2


def _swa_kernel(q_ref, kvp_ref, kvc_ref, kvn_ref, bucket_ref, relb_ref, sink_ref, o_ref, bias_ref):
    b, i = pl.program_id(0), pl.program_id(1)
    nsteps = pl.num_programs(1)
    blk = SWA_BLOCK
    kvw = SWA_HKV * SWA_DH

    @pl.when((b == 0) & (i == 0))
    def _():
        bucket = bucket_ref[...]
        for hq in range(SWA_HQ):
            acc = jnp.full(bucket.shape, NEG_BIG, F32)
            for bk in range(REL_BUCKETS):
                acc = jnp.where(bucket == bk, relb_ref[bk, hq], acc)
            bias_ref[hq * blk:(hq + 1) * blk, :] = acc

    rows = SWA_GROUP * blk
    col = lax.broadcasted_iota(jnp.int32, (1, 3 * blk), 1)
    kv = jnp.concatenate([kvp_ref[0], kvc_ref[0], kvn_ref[0]], axis=0)
    for sb in range(SWA_STEP_BLOCKS):
        qrows = slice(sb * blk, (sb + 1) * blk)
        krows = slice(sb * blk, (sb + 3) * blk)
        outside = jnp.zeros(col.shape, jnp.bool_)
        if sb == 0:
            outside = outside | ((col < blk) & (i == 0))
        if sb == SWA_STEP_BLOCKS - 1:
            outside = outside | ((col >= 2 * blk) & (i == nsteps - 1))
        edge = jnp.where(outside, NEG_BIG, 0.0)
        for h in range(SWA_HKV):
            qs = jnp.concatenate(
                [q_ref[0, qrows, (h * SWA_GROUP + g) * SWA_DH:(h * SWA_GROUP + g + 1) * SWA_DH]
                 for g in range(SWA_GROUP)], axis=0)
            kcat = kv[krows, h * SWA_DH:(h + 1) * SWA_DH]
            vcat = kv[krows, kvw + h * SWA_DH:kvw + (h + 1) * SWA_DH]
            s = _dot_nt(qs, kcat) * SWA_DH ** -0.5 + (bias_ref[h * rows:(h + 1) * rows, :] + edge)
            sink = jnp.concatenate(
                [jnp.full((blk, 1), sink_ref[h * SWA_GROUP + g], F32) for g in range(SWA_GROUP)], axis=0)
            m = jnp.maximum(jnp.max(s, axis=-1, keepdims=True), sink)
            pexp = jnp.exp(s - m)
            denom = jnp.sum(pexp, axis=-1, keepdims=True) + jnp.exp(sink - m)
            o = _dot(pexp.astype(BF16), vcat) / denom
            for g in range(SWA_GROUP):
                osl = slice((h * SWA_GROUP + g) * SWA_DH, (h * SWA_GROUP + g + 1) * SWA_DH)
                o_ref[0, qrows, osl] = o[g * blk:(g + 1) * blk].astype(o_ref.dtype)


def _swa_attention(qkv, rel_bias, sink):
    b, t, _ = qkv.shape
    nb = t // SWA_BLOCK
    sbk = SWA_STEP_BLOCKS
    qw = SWA_HQ * SWA_DH
    kvw2 = 2 * SWA_HKV * SWA_DH
    cblk = qw // kvw2
    return pl.pallas_call(
        _swa_kernel,
        grid=(b, nb // sbk),
        in_specs=[
            pl.BlockSpec((1, sbk * SWA_BLOCK, qw), lambda bi, i: (bi, i, 0)),
            pl.BlockSpec((1, SWA_BLOCK, kvw2), lambda bi, i: (bi, jnp.maximum(i * sbk - 1, 0), cblk)),
            pl.BlockSpec((1, sbk * SWA_BLOCK, kvw2), lambda bi, i: (bi, i, cblk)),
            pl.BlockSpec((1, SWA_BLOCK, kvw2), lambda bi, i: (bi, jnp.minimum((i + 1) * sbk, nb - 1), cblk)),
            pl.BlockSpec((SWA_BLOCK, 3 * SWA_BLOCK), lambda bi, i: (0, 0)),
            pl.BlockSpec(memory_space=pltpu.SMEM),
            pl.BlockSpec(memory_space=pltpu.SMEM),
        ],
        out_specs=pl.BlockSpec((1, sbk * SWA_BLOCK, qw), lambda bi, i: (bi, i, 0)),
        out_shape=jax.ShapeDtypeStruct((b, t, qw), BF16),
        scratch_shapes=[pltpu.VMEM((SWA_HQ * SWA_BLOCK, 3 * SWA_BLOCK), F32)],
        compiler_params=_cparams("arbitrary", "arbitrary"),
    )(qkv, qkv, qkv, qkv, jnp.asarray(_rel_bucket_table()), rel_bias.astype(F32), sink.astype(F32))


def _swa_layer(x, p, j, li):
    b, t, d = x.shape
    n = b * t
    x2 = x.reshape(n, d)
    qkv = _norm_matmul(x2, p['pre_mix_g'][li], p['swa_w_in'][j], BF16, 1024, 1024)
    o = _swa_attention(qkv.reshape(b, t, -1), p['rel_bias'], p['swa_sink'][j])
    y = _matmul_norm_res(o.reshape(n, d), p['swa_w_out'][j], x2, p['post_mix_g'][li], 512)
    return y.reshape(b, t, d)


def _fnet_chan_kernel(x_ref, g_ref, w_ref, o_ref, h_ref):
    gi = pl.program_id(1)

    @pl.when(gi == 0)
    def _():
        h_ref[...] = _rms(x_ref[...], g_ref[...]).astype(BF16)

    hg = h_ref[:, pl.ds(pl.multiple_of(gi * FNET_GW, FNET_GW), FNET_GW)]
    r = _dot(hg, w_ref[...])
    o_ref[0, 0] = r[:, :FNET_GW].astype(o_ref.dtype)
    o_ref[0, 1] = r[:, FNET_GW:].astype(o_ref.dtype)


def _fnet_chan(x, g, w_cs, tm):
    b, t, d = x.shape
    tm = min(tm, t)
    nt = t // tm
    return pl.pallas_call(
        _fnet_chan_kernel,
        grid=(b * nt, FNET_GROUPS),
        in_specs=[
            pl.BlockSpec((tm, d), lambda i, gi: (i, 0)),
            pl.BlockSpec((1, d), lambda i, gi: (0, 0)),
            pl.BlockSpec((FNET_GW, 2 * FNET_GW), lambda i, gi: (0, 0)),
        ],
        out_specs=pl.BlockSpec((1, 2, tm, FNET_GW), lambda i, gi: (i // nt, 0, i % nt, gi)),
        out_shape=jax.ShapeDtypeStruct((b, 2, t, d), BF16),
        scratch_shapes=[pltpu.VMEM((tm, d), BF16)],
        compiler_params=_cparams("parallel", "arbitrary"),
    )(x.reshape(b * t, d), g.reshape(1, d), w_cs)


def _bmm_kernel(a_ref, b_ref, o_ref, acc_ref):
    k = pl.program_id(2)

    @pl.when(k == 0)
    def _():
        acc_ref[...] = jnp.zeros(acc_ref.shape, F32)

    acc_ref[...] += _dot(a_ref[...], b_ref[0])

    @pl.when(k == pl.num_programs(2) - 1)
    def _():
        o_ref[0] = acc_ref[...].astype(o_ref.dtype)


def _shared_lhs_bmm(a, bm, tm, tk):
    m, kdim = a.shape
    b, _, d = bm.shape
    tm = min(tm, m)
    return pl.pallas_call(
        _bmm_kernel,
        grid=(b, m // tm, kdim // tk),
        in_specs=[
            pl.BlockSpec((tm, tk), lambda bi, i, k: (i, k)),
            pl.BlockSpec((1, tk, d), lambda bi, i, k: (bi, k, 0)),
        ],
        out_specs=pl.BlockSpec((1, tm, d), lambda bi, i, k: (bi, i, 0)),
        out_shape=jax.ShapeDtypeStruct((b, m, d), BF16),
        scratch_shapes=[pltpu.VMEM((tm, d), F32)],
        compiler_params=_cparams("parallel", "parallel", "arbitrary"),
    )(a, bm)


def _dft_tables(t):
    def angle(n):
        idx = jnp.arange(n, dtype=jnp.int32)
        return (idx[:, None] * idx[None, :] % n).astype(F32) * (2.0 * math.pi / n)

    ac = angle(FNET_GW)
    at = angle(t)
    scale = 1.0 / math.sqrt(t * FNET_GW)
    w_cs = jnp.concatenate([jnp.cos(ac), jnp.sin(ac)], axis=1).astype(BF16)
    f_t = (jnp.concatenate([jnp.cos(at), -jnp.sin(at)], axis=1) * scale).astype(BF16)
    return w_cs, f_t


def _fnet_layer(x, p, j, li):
    b, t, d = x.shape
    n = b * t
    w_cs, f_t = _dft_tables(t)
    pcs = _fnet_chan(x, p['pre_mix_g'][li], w_cs, 1024)
    y = _shared_lhs_bmm(f_t, pcs.reshape(b, 2 * t, d), 1024, 1024)
    out = _matmul_norm_res(y.reshape(n, d), p['fnet_w_out'][j], x.reshape(n, d), p['post_mix_g'][li], 512)
    return out.reshape(b, t, d)


def _trunk(x, p):
    b, t, d = x.shape
    for i in range(DEPTH):
        kind, j = i % N_MIXERS, i // N_MIXERS
        if kind == 0:
            x = _gated_deltanet_layer(x, p, j, i)
        elif kind == 1:
            x = _swa_layer(x, p, j, i)
        else:
            x = _fnet_layer(x, p, j, i)
        x = _ffn(x.reshape(b * t, d), p['pre_ffn_g'][i], p['ffn_w_gate_up'][i], p['ffn_w_down'][i],
                 p['post_ffn_g'][i], 512, 512).reshape(b, t, d)
    return x


def kernel(x_prompt, x_sample, pre_mix_g, post_mix_g, pre_ffn_g, post_ffn_g, gdn_w_in, gdn_conv_w, gdn_a_log,
           gdn_dt_bias, gdn_norm_g, gdn_w_out, swa_w_in, swa_sink, swa_w_out, rel_bias, fnet_w_out,
           ffn_w_gate_up, ffn_w_down):
    p = dict(
        pre_mix_g=pre_mix_g, post_mix_g=post_mix_g, pre_ffn_g=pre_ffn_g, post_ffn_g=post_ffn_g,
        gdn_w_main=gdn_w_in[:, :, :GDN_MAIN_DIM].astype(BF16),
        gdn_w_gate=gdn_w_in[:, :, GDN_MAIN_DIM:].astype(BF16),
        gdn_conv_w=gdn_conv_w, gdn_a_log=gdn_a_log, gdn_dt_bias=gdn_dt_bias, gdn_norm_g=gdn_norm_g,
        gdn_w_out=gdn_w_out.astype(BF16),
        swa_w_in=swa_w_in.astype(BF16), swa_sink=swa_sink, swa_w_out=swa_w_out.astype(BF16),
        rel_bias=rel_bias, fnet_w_out=fnet_w_out.astype(BF16),
        ffn_w_gate_up=ffn_w_gate_up.astype(BF16), ffn_w_down=ffn_w_down.astype(BF16),
    )
    return _trunk(x_prompt, p), _trunk(x_sample, p)
```

```python
import functools
import math

import numpy as np
import jax
import jax.numpy as jnp
from jax import lax
from jax.experimental import pallas as pl
from jax.experimental.pallas import tpu as pltpu

D_MODEL = 2048
DEPTH = 4
N_MIXERS = 3

GDN_DK = 128
GDN_DV = 128
GDN_HK = D_MODEL // 128
GDN_HV = 2 * GDN_HK
GDN_QK_DIM = GDN_HK * GDN_DK
GDN_V_DIM = GDN_HV * GDN_DV
GDN_CONV_DIM = 2 * GDN_QK_DIM + GDN_V_DIM
GDN_CONV_W = 5
GDN_MAIN_DIM = GDN_CONV_DIM + GDN_V_DIM
GDN_GATE_DIM = 4 * GDN_HV
GDN_CHUNK = 128
GDN_GROUP = 8
GDN_INV_BASE = 16

SWA_DH = 128
SWA_HQ = D_MODEL // SWA_DH
SWA_HKV = 4
SWA_GROUP = SWA_HQ // SWA_HKV
WINDOW = 128
SWA_BLOCK = 128
REL_BUCKETS = 32
REL_MAX_DIST = 128

FNET_GROUPS = 4
FNET_GW = D_MODEL // FNET_GROUPS

FFN_HIDDEN = -(-8 * D_MODEL // (3 * 256)) * 256

RMS_EPS = 1e-6
NEG_BIG = -1e30

LANE = 128
BF16_SUBLANE = 16
VMEM_LIMIT_BYTES = 56 * 1024 * 1024

BF16 = jnp.bfloat16
F32 = jnp.float32


def _cparams(*sem):
    return pltpu.CompilerParams(dimension_semantics=sem, vmem_limit_bytes=VMEM_LIMIT_BYTES)


def _rms(x, g):
    return x * lax.rsqrt(jnp.mean(x * x, axis=-1, keepdims=True) + RMS_EPS) * g


def _silu(x):
    return x * (1.0 / (1.0 + jnp.exp(-x)))


def _dot(a, b):
    return jnp.dot(a, b, preferred_element_type=F32)


def _dot_nt(a, b):
    return lax.dot_general(a, b, (((1,), (1,)), ((), ())), preferred_element_type=F32)


def _norm_matmul_kernel(x_ref, g_ref, w_ref, o_ref, h_ref):
    @pl.when(pl.program_id(1) == 0)
    def _():
        h_ref[...] = _rms(x_ref[...], g_ref[...]).astype(h_ref.dtype)

    o_ref[...] = _dot(h_ref[...], w_ref[...]).astype(o_ref.dtype)


def _norm_matmul(x, g, w, out_dtype, tm, tn):
    n, d = x.shape
    m = w.shape[1]
    tm = min(tm, n)
    return pl.pallas_call(
        _norm_matmul_kernel,
        grid=(n // tm, m // tn),
        in_specs=[
            pl.BlockSpec((tm, d), lambda i, j: (i, 0)),
            pl.BlockSpec((1, d), lambda i, j: (0, 0)),
            pl.BlockSpec((d, tn), lambda i, j: (0, j)),
        ],
        out_specs=pl.BlockSpec((tm, tn), lambda i, j: (i, j)),
        out_shape=jax.ShapeDtypeStruct((n, m), out_dtype),
        scratch_shapes=[pltpu.VMEM((tm, d), BF16)],
        compiler_params=_cparams("parallel", "arbitrary"),
    )(x, g.reshape(1, d), w)


def _matmul_norm_res_kernel(*refs, gated):
    if gated:
        a_ref, z_ref, ng_ref, w_ref, x_ref, g_ref, o_ref = refs
        ng = ng_ref[...]
        parts = []
        for h in range(a_ref.shape[1] // GDN_DV):
            sl = slice(h * GDN_DV, (h + 1) * GDN_DV)
            parts.append((_rms(a_ref[:, sl], ng) * _silu(z_ref[:, sl].astype(F32))).astype(BF16))
        a = jnp.concatenate(parts, axis=1)
    else:
        a_ref, w_ref, x_ref, g_ref, o_ref = refs
        a = a_ref[...].astype(BF16)
    o_ref[...] = x_ref[...] + _rms(_dot(a, w_ref[...]), g_ref[...])


def _matmul_norm_res(a, w, x, g, tm, gate=None):
    n, kdim = a.shape
    d = w.shape[1]
    in_specs = [pl.BlockSpec((tm, kdim), lambda i: (i, 0))]
    args = [a]
    if gate is not None:
        zsrc, zcol0, norm_g = gate
        zblk = zcol0 // kdim
        in_specs += [pl.BlockSpec((tm, kdim), lambda i: (i, zblk)),
                     pl.BlockSpec((1, GDN_DV), lambda i: (0, 0))]
        args += [zsrc, norm_g.reshape(1, GDN_DV)]
    in_specs += [pl.BlockSpec((kdim, d), lambda i: (0, 0), pipeline_mode=pl.Buffered(1)),
                 pl.BlockSpec((tm, d), lambda i: (i, 0)),
                 pl.BlockSpec((1, d), lambda i: (0, 0))]
    args += [w, x, g.reshape(1, d)]
    return pl.pallas_call(
        functools.partial(_matmul_norm_res_kernel, gated=gate is not None),
        grid=(n // tm,),
        in_specs=in_specs,
        out_specs=pl.BlockSpec((tm, d), lambda i: (i, 0)),
        out_shape=jax.ShapeDtypeStruct((n, d), F32),
        compiler_params=_cparams("parallel"),
    )(*args)


def _ffn_kernel(x_ref, g1_ref, wg_ref, wu_ref, wd_ref, g2_ref, o_ref, h_ref, acc_ref):
    f = pl.program_id(1)

    @pl.when(f == 0)
    def _():
        h_ref[...] = _rms(x_ref[...], g1_ref[...]).astype(BF16)
        acc_ref[...] = jnp.zeros(acc_ref.shape, F32)

    h = h_ref[...]
    gate = _dot(h, wg_ref[...])
    up = _dot(h, wu_ref[...])
    acc_ref[...] += _dot((_silu(gate) * up).astype(BF16), wd_ref[...])

    @pl.when(f == pl.num_programs(1) - 1)
    def _():
        o_ref[...] = x_ref[...] + _rms(acc_ref[...], g2_ref[...])


def _ffn(x, g1, w_gu, w_d, g2, tm, tf):
    n, d = x.shape
    fdim = w_d.shape[0]
    nf = fdim // tf
    tm = min(tm, n)
    return pl.pallas_call(
        _ffn_kernel,
        grid=(n // tm, nf),
        in_specs=[
            pl.BlockSpec((tm, d), lambda i, f: (i, 0)),
            pl.BlockSpec((1, d), lambda i, f: (0, 0)),
            pl.BlockSpec((d, tf), lambda i, f: (0, f)),
            pl.BlockSpec((d, tf), lambda i, f: (0, nf + f)),
            pl.BlockSpec((tf, d), lambda i, f: (f, 0)),
            pl.BlockSpec((1, d), lambda i, f: (0, 0)),
        ],
        out_specs=pl.BlockSpec((tm, d), lambda i, f: (i, 0)),
        out_shape=jax.ShapeDtypeStruct((n, d), F32),
        scratch_shapes=[pltpu.VMEM((tm, d), BF16), pltpu.VMEM((tm, d), F32)],
        compiler_params=_cparams("parallel", "arbitrary"),
    )(x, g1.reshape(1, d), w_gu, w_gu, w_d, g2.reshape(1, d))


def _gdn_gate_kernel(ba_ref, nega_ref, dtb_ref, o_ref, *, chunk):
    rows = ba_ref.shape[0]
    lane = lax.broadcasted_iota(jnp.int32, (chunk, GDN_GATE_DIM), 1)
    is_dt = (lane % (2 * GDN_HV)) >= GDN_HV
    is_bwd = lane >= 2 * GDN_HV
    ri = lax.broadcasted_iota(jnp.int32, (chunk, chunk), 0)
    ci = lax.broadcasted_iota(jnp.int32, (chunk, chunk), 1)
    tril = jnp.where(ri >= ci, 1.0, 0.0).astype(BF16)
    triu = jnp.where(ri <= ci, 1.0, 0.0).astype(BF16)
    for c in range(rows // chunk):
        x = ba_ref[c * chunk:(c + 1) * chunk, :]
        beta = 1.0 / (1.0 + jnp.exp(-x))
        y = x + dtb_ref[...]
        sp = jnp.maximum(y, 0.0) + jnp.log1p(jnp.exp(-jnp.abs(y)))
        g = jnp.where(is_dt, nega_ref[...] * sp, 0.0)
        hi = g.astype(BF16)
        lo = (g - hi.astype(F32)).astype(BF16)
        pre = _dot(tril, hi) + _dot(tril, lo)
        suf = _dot(triu, hi) + _dot(triu, lo)
        o_ref[c * chunk:(c + 1) * chunk, :] = jnp.where(is_dt, jnp.where(is_bwd, suf, pre), beta)


def _gdn_gates(ba, a_log, dt_bias, chunk, tt):
    n = ba.shape[0]
    zeros = jnp.zeros((2, GDN_HV), F32)
    nega = jnp.stack([zeros, -jnp.exp(a_log.astype(F32))], axis=1).reshape(1, GDN_GATE_DIM)
    dtb = jnp.stack([zeros, dt_bias.astype(F32)], axis=1).reshape(1, GDN_GATE_DIM)
    return pl.pallas_call(
        functools.partial(_gdn_gate_kernel, chunk=chunk),
        grid=(n // tt,),
        in_specs=[pl.BlockSpec((tt, GDN_GATE_DIM), lambda i: (i, 0)),
                  pl.BlockSpec((1, GDN_GATE_DIM), lambda i: (0, 0)),
                  pl.BlockSpec((1, GDN_GATE_DIM), lambda i: (0, 0))],
        out_specs=pl.BlockSpec((tt, GDN_GATE_DIM), lambda i: (i, 0)),
        out_shape=jax.ShapeDtypeStruct((n, GDN_GATE_DIM), F32),
        compiler_params=_cparams("parallel"),
    )(ba, nega, dtb)


GDN_PREP_ROWS = 128
GDN_CONV_SIDE_TAPS = tuple(w for w in range(GDN_CONV_W) if w != GDN_CONV_W // 2)


def _conv_shift_matrix(halo):
    r = GDN_PREP_ROWS
    pad = GDN_CONV_W // 2
    sh = np.zeros((len(GDN_CONV_SIDE_TAPS) * r, r + 2 * halo), np.float32)
    for k, w in enumerate(GDN_CONV_SIDE_TAPS):
        sh[k * r + np.arange(r), halo + np.arange(r) + w - pad] = 1.0
    return sh


def _gdn_prep_kernel(cur_ref, prev_ref, next_ref, cw_ref, sh_ref, o_ref, ext_ref):
    i = pl.program_id(1)
    j = pl.program_id(2)
    tt, tc = cur_ref.shape[1], cur_ref.shape[2]
    halo = prev_ref.shape[1]
    pad = GDN_CONV_W // 2
    r = GDN_PREP_ROWS
    zero = jnp.zeros((halo, tc), BF16)
    ext_ref[0:halo, :] = jnp.where(i > 0, prev_ref[0], zero)
    ext_ref[halo:halo + tt, :] = cur_ref[0]
    ext_ref[halo + tt:2 * halo + tt, :] = jnp.where(i < pl.num_programs(1) - 1, next_ref[0], zero)
    col0 = j * tc

    def conv_silu(rb):
        window = ext_ref[rb * r:rb * r + r + 2 * halo, :]
        side = _dot(sh_ref[...], window)
        acc = window[halo:halo + r].astype(F32) * cw_ref[pad:pad + 1, :]
        for k, w in enumerate(GDN_CONV_SIDE_TAPS):
            acc = acc + side[k * r:(k + 1) * r] * cw_ref[w:w + 1, :]
        return _silu(acc)

    @pl.when(col0 < 2 * GDN_QK_DIM)
    def _():
        qscale = jnp.where(col0 < GDN_QK_DIM, GDN_DK ** -0.5, 1.0)
        for rb in range(tt // r):
            y = conv_silu(rb)
            for h in range(tc // GDN_DK):
                sl = slice(h * GDN_DK, (h + 1) * GDN_DK)
                yh = y[:, sl]
                inv = lax.rsqrt(jnp.sum(yh * yh, axis=-1, keepdims=True) + 1e-6)
                o_ref[0, rb * r:(rb + 1) * r, sl] = (yh * inv * qscale).astype(o_ref.dtype)

    @pl.when(col0 >= 2 * GDN_QK_DIM)
    def _():
        for rb in range(tt // r):
            o_ref[0, rb * r:(rb + 1) * r, :] = conv_silu(rb).astype(o_ref.dtype)


def _gdn_prep(proj, conv_w, tt, tc):
    b, t, _ = proj.shape
    halo = BF16_SUBLANE
    nh = t // halo
    r = tt // halo
    sh = jnp.asarray(_conv_shift_matrix(halo), BF16)
    return pl.pallas_call(
        _gdn_prep_kernel,
        grid=(b, t // tt, GDN_CONV_DIM // tc),
        in_specs=[
            pl.BlockSpec((1, tt, tc), lambda bi, i, j: (bi, i, j)),
            pl.BlockSpec((1, halo, tc), lambda bi, i, j: (bi, jnp.maximum(i * r - 1, 0), j)),
            pl.BlockSpec((1, halo, tc), lambda bi, i, j: (bi, jnp.minimum((i + 1) * r, nh - 1), j)),
            pl.BlockSpec((GDN_CONV_W, tc), lambda bi, i, j: (0, j)),
            pl.BlockSpec(sh.shape, lambda bi, i, j: (0, 0)),
        ],
        out_specs=pl.BlockSpec((1, tt, tc), lambda bi, i, j: (bi, i, j)),
        out_shape=jax.ShapeDtypeStruct((b, t, GDN_CONV_DIM), BF16),
        scratch_shapes=[pltpu.VMEM((tt + 2 * halo, tc), BF16)],
        compiler_params=_cparams("parallel", "parallel", "parallel"),
    )(proj, proj, proj, conv_w, sh)


def _tri_inverse_masks(c, upper):
    ri = lax.broadcasted_iota(jnp.int32, (c, c), 0)
    ci = lax.broadcasted_iota(jnp.int32, (c, c), 1)
    base_mask = (ri // GDN_INV_BASE) == (ci // GDN_INV_BASE)
    level_masks = []
    b = GDN_INV_BASE
    while b < c:
        mask = ((ri // (2 * b)) == (ci // (2 * b))) & ((ri // b) != (ci // b))
        level_masks.append(_take_blocks(mask, b, 0 if upper else 1))
        b *= 2
    return base_mask, level_masks


def _take_blocks(x, b, parity):
    return jnp.concatenate([x[s:s + b] for s in range(parity * b, x.shape[0], 2 * b)], axis=0)


def _merge_level(p, low, mask, b, upper):
    c = p.shape[0]
    par = 0 if upper else 1
    nblk = c // (2 * b)
    pb = p.astype(BF16)
    off = jnp.where(mask, _take_blocks(low, b, par), 0.0).astype(BF16)
    x = _dot(off, pb).astype(BF16)
    yield
    zero = jnp.zeros((b, c), BF16)
    pieces = []
    for i in range(nblk):
        xi = x[i * b:(i + 1) * b]
        pieces += [xi, zero] if upper else [zero, xi]
    y = _dot(_take_blocks(pb, b, par), jnp.concatenate(pieces, axis=0))
    yield
    rows = []
    for i in range(nblk):
        keep = p[(2 * i + 1 - par) * b:(2 * i + 2 - par) * b]
        upd = p[(2 * i + par) * b:(2 * i + par + 1) * b] - y[i * b:(i + 1) * b]
        rows += [upd, keep] if upper else [keep, upd]
    yield jnp.concatenate(rows, axis=0)


def _unit_tri_inverse(low, eye, masks, upper):
    c = low.shape[0]
    base_mask, level_masks = masks
    m0 = jnp.where(base_mask, -low, 0.0)
    p = eye + m0
    mb = m0.astype(BF16)
    m = _dot(mb, mb)
    yield
    steps = int(math.log2(GDN_INV_BASE))
    for s in range(2, steps + 1):
        mb = m.astype(BF16)
        if s < steps:
            r = _dot(mb, jnp.concatenate([mb, p.astype(BF16)], axis=1))
            m = r[:, :c]
            p = p + r[:, c:]
        else:
            p = p + _dot(mb, p.astype(BF16))
        yield
    b = GDN_INV_BASE
    for mask in level_masks:
        for out in _merge_level(p, low, mask, b, upper):
            if out is None:
                yield
        p = out
        b *= 2
    return p


def _delta_chain(qf, kf, kt, qk, kk, vf, s, beta, gcum, gcum_r, glast, eye, incl, strict, masks, upper):
    chunk = qf.shape[0]
    dec = jnp.where(incl, jnp.exp(gcum - gcum_r), 0.0)
    low = jnp.where(strict, kk * dec, 0.0) * beta
    tinv = yield from _unit_tri_inverse(low, eye, masks, upper)
    eg = jnp.exp(gcum)
    rhs = jnp.concatenate([vf * beta, kf * beta * eg], axis=1).astype(BF16)
    sol = _dot(tinv.astype(BF16), rhs)
    yield
    u, w = sol[:, :GDN_DV], sol[:, GDN_DV:]
    ws_qs = _dot(jnp.concatenate([w, qf * eg], axis=0).astype(BF16), s.astype(BF16))
    yield
    v_new = (u - ws_qs[:chunk]).astype(BF16)
    kend_t = kt * jnp.exp(glast - gcum_r)
    r = _dot(jnp.concatenate([(qk * dec).astype(BF16), kend_t.astype(BF16)], axis=0), v_new)
    yield
    yield ws_qs[chunk:] + r[:chunk], s * jnp.exp(glast) + r[chunk:]


def _run_interleaved(gens):
    results = [None] * len(gens)
    live = list(range(len(gens)))
    while live:
        nxt = []
        for i in live:
            try:
                results[i] = next(gens[i])
                nxt.append(i)
            except StopIteration:
                pass
        live = nxt
    return results


def _gdn_core_kernel(q_ref, k_ref, v_ref, gc_ref, gr_ref, o_ref, s_ref, *, group, chunk):
    t = q_ref.shape[1]
    nc = t // chunk
    ri = lax.broadcasted_iota(jnp.int32, (chunk, chunk), 0)
    ci = lax.broadcasted_iota(jnp.int32, (chunk, chunk), 1)
    eye = jnp.where(ri == ci, 1.0, 0.0).astype(F32)
    incl = (ri >= ci, ri <= ci)
    strict = (ri > ci, ri < ci)
    inv_masks = (_tri_inverse_masks(chunk, False), _tri_inverse_masks(chunk, True))
    s_ref[...] = jnp.zeros(s_ref.shape, F32)
    o_ref[...] = jnp.zeros(o_ref.shape, F32)

    def body(c, carry):
        chains = []
        rows = []
        for d in range(2):
            cc = c if d == 0 else nc - 1 - c
            rows.append(pl.ds(pl.multiple_of(cc * chunk, chunk), chunk))
            gcol = gc_ref[0, 0, rows[d], :]
            grow = gr_ref[0, 0, :, rows[d]]
            edge = chunk - 1 if d == 0 else 0
            for kh in range(group // 2):
                ksl = slice(kh * GDN_DK, (kh + 1) * GDN_DK)
                qb = q_ref[0, rows[d], ksl]
                kb = k_ref[0, rows[d], ksl]
                qf = qb.astype(F32)
                kf = kb.astype(F32)
                gram = _dot_nt(jnp.concatenate([qb, kb], axis=0), kb)
                kt = kf.T
                for j in (2 * kh, 2 * kh + 1):
                    cb = d * 2 * group + j
                    cg = cb + group
                    vf = v_ref[0, rows[d], j * GDN_DV:(j + 1) * GDN_DV].astype(F32)
                    chains.append(_delta_chain(
                        qf, kf, kt, gram[:chunk], gram[chunk:], vf, s_ref[d * group + j],
                        gcol[:, cb:cb + 1], gcol[:, cg:cg + 1], grow[cg:cg + 1, :], gcol[edge:edge + 1, cg:cg + 1],
                        eye, incl[d], strict[d], inv_masks[d], d == 1))
        outs = _run_interleaved(chains)
        for idx, (o_rows, s_new) in enumerate(outs):
            d, j = divmod(idx, group)
            o_ref[0, rows[d], j * GDN_DV:(j + 1) * GDN_DV] += o_rows
            s_ref[d * group + j] = s_new
        return carry

    lax.fori_loop(0, nc, body, 0)


def _gdn_core(qkv, gcol, grow, group, chunk):
    b, t, _ = qkv.shape
    ng = GDN_HV // group
    qw = (group // 2) * GDN_DK
    vw = group * GDN_DV
    kblk0 = GDN_QK_DIM // qw
    vblk0 = 2 * GDN_QK_DIM // vw
    once = pl.Buffered(1)
    return pl.pallas_call(
        functools.partial(_gdn_core_kernel, group=group, chunk=chunk),
        grid=(b, ng),
        in_specs=[
            pl.BlockSpec((1, t, qw), lambda bi, hg: (bi, 0, hg), pipeline_mode=once),
            pl.BlockSpec((1, t, qw), lambda bi, hg: (bi, 0, kblk0 + hg), pipeline_mode=once),
            pl.BlockSpec((1, t, vw), lambda bi, hg: (bi, 0, vblk0 + hg), pipeline_mode=once),
            pl.BlockSpec((1, 1, t, 4 * group), lambda bi, hg: (bi, hg, 0, 0)),
            pl.BlockSpec((1, 1, 4 * group, t), lambda bi, hg: (bi, hg, 0, 0)),
        ],
        out_specs=pl.BlockSpec((1, t, vw), lambda bi, hg: (bi, 0, hg), pipeline_mode=once),
        out_shape=jax.ShapeDtypeStruct((b, t, GDN_V_DIM), F32),
        scratch_shapes=[pltpu.VMEM((2 * group, GDN_DK, GDN_DV), F32)],
        compiler_params=_cparams("parallel", "parallel"),
    )(qkv, qkv, qkv, gcol, grow)


def _gated_deltanet_layer(x, p, j, li):
    b, t, d = x.shape
    n = b * t
    x2 = x.reshape(n, d)
    g_pre = p['pre_mix_g'][li]
    proj = _norm_matmul(x2, g_pre, p['gdn_w_main'][j], BF16, 1024, 2048)
    ba = _norm_matmul(x2, g_pre, p['gdn_w_gate'][j], F32, 1024, GDN_GATE_DIM)
    gates = _gdn_gates(ba, p['gdn_a_log'][j], p['gdn_dt_bias'][j], GDN_CHUNK, 512)
    grp = GDN_GROUP
    ng = GDN_HV // grp
    gcol = gates.reshape(b, t, 4, ng, grp).transpose(0, 3, 1, 2, 4).reshape(b, ng, t, 4 * grp)
    grow = jnp.swapaxes(gcol, 2, 3)
    qkv = _gdn_prep(proj.reshape(b, t, GDN_MAIN_DIM), p['gdn_conv_w'][j], 1024, 1024)
    o = _gdn_core(qkv, gcol, grow, grp, GDN_CHUNK)
    y = _matmul_norm_res(o.reshape(n, GDN_V_DIM), p['gdn_w_out'][j], x2, p['post_mix_g'][li], 256,
                         gate=(proj, GDN_CONV_DIM, p['gdn_norm_g'][j]))
    return y.reshape(b, t, d)


def _rel_bucket_table():
    qi = np.arange(SWA_BLOCK)[:, None]
    kj = np.arange(3 * SWA_BLOCK)[None, :]
    rel = kj - SWA_BLOCK - qi
    half = REL_BUCKETS // 2
    max_exact = half // 2
    nabs = np.abs(rel)
    large = max_exact + (np.log(np.maximum(nabs, 1).astype(np.float32) / max_exact)
                         / math.log(REL_MAX_DIST / max_exact) * (half - max_exact)).astype(np.int32)
    large = np.minimum(large, half - 1)
    bucket = np.where(rel > 0, half, 0) + np.where(nabs < max_exact, nabs, large)
    return np.where(nabs <= WINDOW, bucket, -1).astype(np.int32)


SWA_STEP_BLOCKS = 2


def _swa_kernel(q_ref, kvp_ref, kvc_ref, kvn_ref, bucket_ref, relb_ref, sink_ref, o_ref, bias_ref):
    b, i = pl.program_id(0), pl.program_id(1)
    nsteps = pl.num_programs(1)
    blk = SWA_BLOCK
    kvw = SWA_HKV * SWA_DH

    @pl.when((b == 0) & (i == 0))
    def _():
        bucket = bucket_ref[...]
        for hq in range(SWA_HQ):
            acc = jnp.full(bucket.shape, NEG_BIG, F32)
            for bk in range(REL_BUCKETS):
                acc = jnp.where(bucket == bk, relb_ref[bk, hq], acc)
            bias_ref[hq * blk:(hq + 1) * blk, :] = acc

    rows = SWA_GROUP * blk
    col = lax.broadcasted_iota(jnp.int32, (1, 3 * blk), 1)
    kv = jnp.concatenate([kvp_ref[0], kvc_ref[0], kvn_ref[0]], axis=0)

    def unit(qs, kcat, vcat, bias, sink):
        s = _dot_nt(qs, kcat) * SWA_DH ** -0.5 + bias
        yield
        m = jnp.maximum(jnp.max(s, axis=-1, keepdims=True), sink)
        yield
        pexp = jnp.exp(s - m)
        denom = jnp.sum(pexp, axis=-1, keepdims=True) + jnp.exp(sink - m)
        yield
        yield _dot(pexp.astype(BF16), vcat) / denom

    units = []
    for sb in range(SWA_STEP_BLOCKS):
        qrows = slice(sb * blk, (sb + 1) * blk)
        krows = slice(sb * blk, (sb + 3) * blk)
        outside = jnp.zeros(col.shape, jnp.bool_)
        if sb == 0:
            outside = outside | ((col < blk) & (i == 0))
        if sb == SWA_STEP_BLOCKS - 1:
            outside = outside | ((col >= 2 * blk) & (i == nsteps - 1))
        edge = jnp.where(outside, NEG_BIG, 0.0)
        for h in range(SWA_HKV):
            qs = jnp.concatenate(
                [q_ref[0, qrows, (h * SWA_GROUP + g) * SWA_DH:(h * SWA_GROUP + g + 1) * SWA_DH]
                 for g in range(SWA_GROUP)], axis=0)
            sink = jnp.concatenate(
                [jnp.full((blk, 1), sink_ref[h * SWA_GROUP + g], F32) for g in range(SWA_GROUP)], axis=0)
            units.append(unit(qs, kv[krows, h * SWA_DH:(h + 1) * SWA_DH],
                              kv[krows, kvw + h * SWA_DH:kvw + (h + 1) * SWA_DH],
                              bias_ref[h * rows:(h + 1) * rows, :] + edge, sink))
    for idx, o in enumerate(_run_interleaved(units)):
        sb, h = divmod(idx, SWA_HKV)
        for g in range(SWA_GROUP):
            osl = slice((h * SWA_GROUP + g) * SWA_DH, (h * SWA_GROUP + g + 1) * SWA_DH)
            o_ref[0, sb * blk:(sb + 1) * blk, osl] = o[g * blk:(g + 1) * blk].astype(o_ref.dtype)


def _swa_attention(qkv, rel_bias, sink):
    b, t, _ = qkv.shape
    nb = t // SWA_BLOCK
    sbk = SWA_STEP_BLOCKS
    qw = SWA_HQ * SWA_DH
    kvw2 = 2 * SWA_HKV * SWA_DH
    cblk = qw // kvw2
    return pl.pallas_call(
        _swa_kernel,
        grid=(b, nb // sbk),
        in_specs=[
            pl.BlockSpec((1, sbk * SWA_BLOCK, qw), lambda bi, i: (bi, i, 0)),
            pl.BlockSpec((1, SWA_BLOCK, kvw2), lambda bi, i: (bi, jnp.maximum(i * sbk - 1, 0), cblk)),
            pl.BlockSpec((1, sbk * SWA_BLOCK, kvw2), lambda bi, i: (bi, i, cblk)),
            pl.BlockSpec((1, SWA_BLOCK, kvw2), lambda bi, i: (bi, jnp.minimum((i + 1) * sbk, nb - 1), cblk)),
            pl.BlockSpec((SWA_BLOCK, 3 * SWA_BLOCK), lambda bi, i: (0, 0)),
            pl.BlockSpec(memory_space=pltpu.SMEM),
            pl.BlockSpec(memory_space=pltpu.SMEM),
        ],
        out_specs=pl.BlockSpec((1, sbk * SWA_BLOCK, qw), lambda bi, i: (bi, i, 0)),
        out_shape=jax.ShapeDtypeStruct((b, t, qw), BF16),
        scratch_shapes=[pltpu.VMEM((SWA_HQ * SWA_BLOCK, 3 * SWA_BLOCK), F32)],
        compiler_params=_cparams("arbitrary", "arbitrary"),
    )(qkv, qkv, qkv, qkv, jnp.asarray(_rel_bucket_table()), rel_bias.astype(F32), sink.astype(F32))


def _swa_layer(x, p, j, li):
    b, t, d = x.shape
    n = b * t
    x2 = x.reshape(n, d)
    qkv = _norm_matmul(x2, p['pre_mix_g'][li], p['swa_w_in'][j], BF16, 1024, 1024)
    o = _swa_attention(qkv.reshape(b, t, -1), p['rel_bias'], p['swa_sink'][j])
    y = _matmul_norm_res(o.reshape(n, d), p['swa_w_out'][j], x2, p['post_mix_g'][li], 512)
    return y.reshape(b, t, d)


def _fnet_chan_kernel(x_ref, g_ref, w_ref, o_ref, h_ref):
    gi = pl.program_id(1)

    @pl.when(gi == 0)
    def _():
        h_ref[...] = _rms(x_ref[...], g_ref[...]).astype(BF16)

    hg = h_ref[:, pl.ds(pl.multiple_of(gi * FNET_GW, FNET_GW), FNET_GW)]
    r = _dot(hg, w_ref[...])
    o_ref[0, 0] = r[:, :FNET_GW].astype(o_ref.dtype)
    o_ref[0, 1] = r[:, FNET_GW:].astype(o_ref.dtype)


def _fnet_chan(x, g, w_cs, tm):
    b, t, d = x.shape
    tm = min(tm, t)
    nt = t // tm
    return pl.pallas_call(
        _fnet_chan_kernel,
        grid=(b * nt, FNET_GROUPS),
        in_specs=[
            pl.BlockSpec((tm, d), lambda i, gi: (i, 0)),
            pl.BlockSpec((1, d), lambda i, gi: (0, 0)),
            pl.BlockSpec((FNET_GW, 2 * FNET_GW), lambda i, gi: (0, 0)),
        ],
        out_specs=pl.BlockSpec((1, 2, tm, FNET_GW), lambda i, gi: (i // nt, 0, i % nt, gi)),
        out_shape=jax.ShapeDtypeStruct((b, 2, t, d), BF16),
        scratch_shapes=[pltpu.VMEM((tm, d), BF16)],
        compiler_params=_cparams("parallel", "arbitrary"),
    )(x.reshape(b * t, d), g.reshape(1, d), w_cs)


def _bmm_kernel(a_ref, b_ref, o_ref, acc_ref):
    k = pl.program_id(2)

    @pl.when(k == 0)
    def _():
        acc_ref[...] = jnp.zeros(acc_ref.shape, F32)

    acc_ref[...] += _dot(a_ref[...], b_ref[0])

    @pl.when(k == pl.num_programs(2) - 1)
    def _():
        o_ref[0] = acc_ref[...].astype(o_ref.dtype)


def _shared_lhs_bmm(a, bm, tm, tk):
    m, kdim = a.shape
    b, _, d = bm.shape
    tm = min(tm, m)
    return pl.pallas_call(
        _bmm_kernel,
        grid=(b, m // tm, kdim // tk),
        in_specs=[
            pl.BlockSpec((tm, tk), lambda bi, i, k: (i, k)),
            pl.BlockSpec((1, tk, d), lambda bi, i, k: (bi, k, 0)),
        ],
        out_specs=pl.BlockSpec((1, tm, d), lambda bi, i, k: (bi, i, 0)),
        out_shape=jax.ShapeDtypeStruct((b, m, d), BF16),
        scratch_shapes=[pltpu.VMEM((tm, d), F32)],
        compiler_params=_cparams("parallel", "parallel", "arbitrary"),
    )(a, bm)


def _dft_tables(t):
    def angle(n):
        idx = jnp.arange(n, dtype=jnp.int32)
        return (idx[:, None] * idx[None, :] % n).astype(F32) * (2.0 * math.pi / n)

    ac = angle(FNET_GW)
    at = angle(t)
    scale = 1.0 / math.sqrt(t * FNET_GW)
    w_cs = jnp.concatenate([jnp.cos(ac), jnp.sin(ac)], axis=1).astype(BF16)
    f_t = (jnp.concatenate([jnp.cos(at), -jnp.sin(at)], axis=1) * scale).astype(BF16)
    return w_cs, f_t


def _fnet_layer(x, p, j, li):
    b, t, d = x.shape
    n = b * t
    w_cs, f_t = _dft_tables(t)
    pcs = _fnet_chan(x, p['pre_mix_g'][li], w_cs, 1024)
    y = _shared_lhs_bmm(f_t, pcs.reshape(b, 2 * t, d), 1024, 1024)
    out = _matmul_norm_res(y.reshape(n, d), p['fnet_w_out'][j], x.reshape(n, d), p['post_mix_g'][li], 512)
    return out.reshape(b, t, d)


def _trunk(x, p):
    b, t, d = x.shape
    for i in range(DEPTH):
        kind, j = i % N_MIXERS, i // N_MIXERS
        if kind == 0:
            x = _gated_deltanet_layer(x, p, j, i)
        elif kind == 1:
            x = _swa_layer(x, p, j, i)
        else:
            x = _fnet_layer(x, p, j, i)
        x = _ffn(x.reshape(b * t, d), p['pre_ffn_g'][i], p['ffn_w_gate_up'][i], p['ffn_w_down'][i],
                 p['post_ffn_g'][i], 512, 512).reshape(b, t, d)
    return x


def kernel(x_prompt, x_sample, pre_mix_g, post_mix_g, pre_ffn_g, post_ffn_g, gdn_w_in, gdn_conv_w, gdn_a_log,
           gdn_dt_bias, gdn_norm_g, gdn_w_out, swa_w_in, swa_sink, swa_w_out, rel_bias, fnet_w_out,
           ffn_w_gate_up, ffn_w_down):
    p = dict(
        pre_mix_g=pre_mix_g, post_mix_g=post_mix_g, pre_ffn_g=pre_ffn_g, post_ffn_g=post_ffn_g,
        gdn_w_main=gdn_w_in[:, :, :GDN_MAIN_DIM].astype(BF16),
        gdn_w_gate=gdn_w_in[:, :, GDN_MAIN_DIM:].astype(BF16),
        gdn_conv_w=gdn_conv_w, gdn_a_log=gdn_a_log, gdn_dt_bias=gdn_dt_bias, gdn_norm_g=gdn_norm_g,
        gdn_w_out=gdn_w_out.astype(BF16),
        swa_w_in=swa_w_in.astype(BF16), swa_sink=swa_sink, swa_w_out=swa_w_out.astype(BF16),
        rel_bias=rel_bias, fnet_w_out=fnet_w_out.astype(BF16),
        ffn_w_gate_up=ffn_w_gate_up.astype(BF16), ffn_w_down=ffn_w_down.astype(BF16),
    )
    return _trunk(x_prompt, p), _trunk(x_sample, p)
```

```python
import functools
import math

import numpy as np
import jax
import jax.numpy as jnp
from jax import lax
from jax.experimental import pallas as pl
from jax.experimental.pallas import tpu as pltpu

D_MODEL = 2048
DEPTH = 4
N_MIXERS = 3

GDN_DK = 128
GDN_DV = 128
GDN_HK = D_MODEL // 128
GDN_HV = 2 * GDN_HK
GDN_QK_DIM = GDN_HK * GDN_DK
GDN_V_DIM = GDN_HV * GDN_DV
GDN_CONV_DIM = 2 * GDN_QK_DIM + GDN_V_DIM
GDN_CONV_W = 5
GDN_MAIN_DIM = GDN_CONV_DIM + GDN_V_DIM
GDN_GATE_DIM = 4 * GDN_HV
GDN_CHUNK = 128
GDN_GROUP = 8
GDN_INV_BASE = 16

SWA_DH = 128
SWA_HQ = D_MODEL // SWA_DH
SWA_HKV = 4
SWA_GROUP = SWA_HQ // SWA_HKV
WINDOW = 128
SWA_BLOCK = 128
REL_BUCKETS = 32
REL_MAX_DIST = 128

FNET_GROUPS = 4
FNET_GW = D_MODEL // FNET_GROUPS

FFN_HIDDEN = -(-8 * D_MODEL // (3 * 256)) * 256
FFN_TF = 512

RMS_EPS = 1e-6
NEG_BIG = -1e30

LANE = 128
BF16_SUBLANE = 16
VMEM_LIMIT_BYTES = 56 * 1024 * 1024

BF16 = jnp.bfloat16
F32 = jnp.float32


def _cparams(*sem):
    return pltpu.CompilerParams(dimension_semantics=sem, vmem_limit_bytes=VMEM_LIMIT_BYTES)


def _rms(x, g):
    return x * lax.rsqrt(jnp.mean(x * x, axis=-1, keepdims=True) + RMS_EPS) * g


def _silu(x):
    return x * (1.0 / (1.0 + jnp.exp(-x)))


def _dot(a, b):
    return jnp.dot(a, b, preferred_element_type=F32)


def _dot_nt(a, b):
    return lax.dot_general(a, b, (((1,), (1,)), ((), ())), preferred_element_type=F32)


def _norm_matmul_kernel(x_ref, g_ref, w_ref, o_ref, h_ref):
    @pl.when(pl.program_id(1) == 0)
    def _():
        h_ref[...] = _rms(x_ref[...], g_ref[...]).astype(h_ref.dtype)

    o_ref[...] = _dot(h_ref[...], w_ref[...]).astype(o_ref.dtype)


def _norm_matmul(x, g, w, out_dtype, tm, tn):
    n, d = x.shape
    m = w.shape[1]
    tm = min(tm, n)
    return pl.pallas_call(
        _norm_matmul_kernel,
        grid=(n // tm, m // tn),
        in_specs=[
            pl.BlockSpec((tm, d), lambda i, j: (i, 0)),
            pl.BlockSpec((1, d), lambda i, j: (0, 0)),
            pl.BlockSpec((d, tn), lambda i, j: (0, j)),
        ],
        out_specs=pl.BlockSpec((tm, tn), lambda i, j: (i, j)),
        out_shape=jax.ShapeDtypeStruct((n, m), out_dtype),
        scratch_shapes=[pltpu.VMEM((tm, d), BF16)],
        compiler_params=_cparams("parallel", "arbitrary"),
    )(x, g.reshape(1, d), w)


def _matmul_norm_res_kernel(*refs, gated):
    if gated:
        a_ref, z_ref, ng_ref, w_ref, x_ref, g_ref, o_ref = refs
        ng = ng_ref[...]
        parts = []
        for h in range(a_ref.shape[1] // GDN_DV):
            sl = slice(h * GDN_DV, (h + 1) * GDN_DV)
            parts.append((_rms(a_ref[:, sl], ng) * _silu(z_ref[:, sl].astype(F32))).astype(BF16))
        a = jnp.concatenate(parts, axis=1)
    else:
        a_ref, w_ref, x_ref, g_ref, o_ref = refs
        a = a_ref[...].astype(BF16)
    o_ref[...] = x_ref[...] + _rms(_dot(a, w_ref[...]), g_ref[...])


def _matmul_norm_res(a, w, x, g, tm, gate=None):
    n, kdim = a.shape
    d = w.shape[1]
    in_specs = [pl.BlockSpec((tm, kdim), lambda i: (i, 0))]
    args = [a]
    if gate is not None:
        zsrc, zcol0, norm_g = gate
        zblk = zcol0 // kdim
        in_specs += [pl.BlockSpec((tm, kdim), lambda i: (i, zblk)),
                     pl.BlockSpec((1, GDN_DV), lambda i: (0, 0))]
        args += [zsrc, norm_g.reshape(1, GDN_DV)]
    in_specs += [pl.BlockSpec((kdim, d), lambda i: (0, 0), pipeline_mode=pl.Buffered(1)),
                 pl.BlockSpec((tm, d), lambda i: (i, 0)),
                 pl.BlockSpec((1, d), lambda i: (0, 0))]
    args += [w, x, g.reshape(1, d)]
    return pl.pallas_call(
        functools.partial(_matmul_norm_res_kernel, gated=gate is not None),
        grid=(n // tm,),
        in_specs=in_specs,
        out_specs=pl.BlockSpec((tm, d), lambda i: (i, 0)),
        out_shape=jax.ShapeDtypeStruct((n, d), F32),
        compiler_params=_cparams("parallel"),
    )(*args)


def _ffn_kernel(x_ref, g1_ref, wgu_ref, wd_ref, g2_ref, o_ref, h_ref, acc_ref):
    f = pl.program_id(1)
    tf = wd_ref.shape[0]

    @pl.when(f == 0)
    def _():
        h_ref[...] = _rms(x_ref[...], g1_ref[...]).astype(BF16)
        acc_ref[...] = jnp.zeros(acc_ref.shape, F32)

    gu = _dot(h_ref[...], wgu_ref[...])
    gate, up = gu[:, :tf], gu[:, tf:]
    acc_ref[...] += _dot((_silu(gate) * up).astype(BF16), wd_ref[...])

    @pl.when(f == pl.num_programs(1) - 1)
    def _():
        o_ref[...] = x_ref[...] + _rms(acc_ref[...], g2_ref[...])


def _interleave_gate_up(w_gu, tf):
    *lead, d, f2 = w_gu.shape
    nf = f2 // (2 * tf)
    nl = len(lead)
    w = w_gu.reshape(*lead, d, 2, nf, tf)
    return jnp.swapaxes(w, nl + 1, nl + 2).reshape(*lead, d, f2)


def _ffn(x, g1, w_gu, w_d, g2, tm, tf):
    n, d = x.shape
    fdim = w_d.shape[0]
    nf = fdim // tf
    tm = min(tm, n)
    return pl.pallas_call(
        _ffn_kernel,
        grid=(n // tm, nf),
        in_specs=[
            pl.BlockSpec((tm, d), lambda i, f: (i, 0)),
            pl.BlockSpec((1, d), lambda i, f: (0, 0)),
            pl.BlockSpec((d, 2 * tf), lambda i, f: (0, f)),
            pl.BlockSpec((tf, d), lambda i, f: (f, 0)),
            pl.BlockSpec((1, d), lambda i, f: (0, 0)),
        ],
        out_specs=pl.BlockSpec((tm, d), lambda i, f: (i, 0)),
        out_shape=jax.ShapeDtypeStruct((n, d), F32),
        scratch_shapes=[pltpu.VMEM((tm, d), BF16), pltpu.VMEM((tm, d), F32)],
        compiler_params=_cparams("parallel", "arbitrary"),
    )(x, g1.reshape(1, d), w_gu, w_d, g2.reshape(1, d))


def _gdn_gate_kernel(ba_ref, nega_ref, dtb_ref, o_ref, *, chunk):
    rows = ba_ref.shape[0]
    lane = lax.broadcasted_iota(jnp.int32, (chunk, GDN_GATE_DIM), 1)
    is_dt = (lane % (2 * GDN_HV)) >= GDN_HV
    is_bwd = lane >= 2 * GDN_HV
    ri = lax.broadcasted_iota(jnp.int32, (chunk, chunk), 0)
    ci = lax.broadcasted_iota(jnp.int32, (chunk, chunk), 1)
    tril = jnp.where(ri >= ci, 1.0, 0.0).astype(BF16)
    triu = jnp.where(ri <= ci, 1.0, 0.0).astype(BF16)
    for c in range(rows // chunk):
        x = ba_ref[c * chunk:(c + 1) * chunk, :]
        beta = 1.0 / (1.0 + jnp.exp(-x))
        y = x + dtb_ref[...]
        sp = jnp.maximum(y, 0.0) + jnp.log1p(jnp.exp(-jnp.abs(y)))
        g = jnp.where(is_dt, nega_ref[...] * sp, 0.0)
        hi = g.astype(BF16)
        lo = (g - hi.astype(F32)).astype(BF16)
        pre = _dot(tril, hi) + _dot(tril, lo)
        suf = _dot(triu, hi) + _dot(triu, lo)
        o_ref[c * chunk:(c + 1) * chunk, :] = jnp.where(is_dt, jnp.where(is_bwd, suf, pre), beta)


def _gdn_gates(ba, a_log, dt_bias, chunk, tt):
    n = ba.shape[0]
    zeros = jnp.zeros((2, GDN_HV), F32)
    nega = jnp.stack([zeros, -jnp.exp(a_log.astype(F32))], axis=1).reshape(1, GDN_GATE_DIM)
    dtb = jnp.stack([zeros, dt_bias.astype(F32)], axis=1).reshape(1, GDN_GATE_DIM)
    return pl.pallas_call(
        functools.partial(_gdn_gate_kernel, chunk=chunk),
        grid=(n // tt,),
        in_specs=[pl.BlockSpec((tt, GDN_GATE_DIM), lambda i: (i, 0)),
                  pl.BlockSpec((1, GDN_GATE_DIM), lambda i: (0, 0)),
                  pl.BlockSpec((1, GDN_GATE_DIM), lambda i: (0, 0))],
        out_specs=pl.BlockSpec((tt, GDN_GATE_DIM), lambda i: (i, 0)),
        out_shape=jax.ShapeDtypeStruct((n, GDN_GATE_DIM), F32),
        compiler_params=_cparams("parallel"),
    )(ba, nega, dtb)


GDN_PREP_ROWS = 128
GDN_CONV_SIDE_TAPS = tuple(w for w in range(GDN_CONV_W) if w != GDN_CONV_W // 2)


def _conv_shift_matrix(halo):
    r = GDN_PREP_ROWS
    pad = GDN_CONV_W // 2
    sh = np.zeros((len(GDN_CONV_SIDE_TAPS) * r, r + 2 * halo), np.float32)
    for k, w in enumerate(GDN_CONV_SIDE_TAPS):
        sh[k * r + np.arange(r), halo + np.arange(r) + w - pad] = 1.0
    return sh


def _gdn_prep_kernel(cur_ref, prev_ref, next_ref, cw_ref, sh_ref, o_ref, ext_ref):
    i = pl.program_id(1)
    j = pl.program_id(2)
    tt, tc = cur_ref.shape[1], cur_ref.shape[2]
    halo = prev_ref.shape[1]
    pad = GDN_CONV_W // 2
    r = GDN_PREP_ROWS
    zero = jnp.zeros((halo, tc), BF16)
    ext_ref[0:halo, :] = jnp.where(i > 0, prev_ref[0], zero)
    ext_ref[halo:halo + tt, :] = cur_ref[0]
    ext_ref[halo + tt:2 * halo + tt, :] = jnp.where(i < pl.num_programs(1) - 1, next_ref[0], zero)
    col0 = j * tc

    def conv_silu(rb):
        window = ext_ref[rb * r:rb * r + r + 2 * halo, :]
        side = _dot(sh_ref[...], window)
        acc = window[halo:halo + r].astype(F32) * cw_ref[pad:pad + 1, :]
        for k, w in enumerate(GDN_CONV_SIDE_TAPS):
            acc = acc + side[k * r:(k + 1) * r] * cw_ref[w:w + 1, :]
        return _silu(acc)

    @pl.when(col0 < 2 * GDN_QK_DIM)
    def _():
        qscale = jnp.where(col0 < GDN_QK_DIM, GDN_DK ** -0.5, 1.0)
        for rb in range(tt // r):
            y = conv_silu(rb)
            for h in range(tc // GDN_DK):
                sl = slice(h * GDN_DK, (h + 1) * GDN_DK)
                yh = y[:, sl]
                inv = lax.rsqrt(jnp.sum(yh * yh, axis=-1, keepdims=True) + 1e-6)
                o_ref[0, rb * r:(rb + 1) * r, sl] = (yh * inv * qscale).astype(o_ref.dtype)

    @pl.when(col0 >= 2 * GDN_QK_DIM)
    def _():
        for rb in range(tt // r):
            o_ref[0, rb * r:(rb + 1) * r, :] = conv_silu(rb).astype(o_ref.dtype)


def _gdn_prep(proj, conv_w, tt, tc):
    b, t, _ = proj.shape
    halo = BF16_SUBLANE
    nh = t // halo
    r = tt // halo
    sh = jnp.asarray(_conv_shift_matrix(halo), BF16)
    return pl.pallas_call(
        _gdn_prep_kernel,
        grid=(b, t // tt, GDN_CONV_DIM // tc),
        in_specs=[
            pl.BlockSpec((1, tt, tc), lambda bi, i, j: (bi, i, j)),
            pl.BlockSpec((1, halo, tc), lambda bi, i, j: (bi, jnp.maximum(i * r - 1, 0), j)),
            pl.BlockSpec((1, halo, tc), lambda bi, i, j: (bi, jnp.minimum((i + 1) * r, nh - 1), j)),
            pl.BlockSpec((GDN_CONV_W, tc), lambda bi, i, j: (0, j)),
            pl.BlockSpec(sh.shape, lambda bi, i, j: (0, 0)),
        ],
        out_specs=pl.BlockSpec((1, tt, tc), lambda bi, i, j: (bi, i, j)),
        out_shape=jax.ShapeDtypeStruct((b, t, GDN_CONV_DIM), BF16),
        scratch_shapes=[pltpu.VMEM((tt + 2 * halo, tc), BF16)],
        compiler_params=_cparams("parallel", "parallel", "parallel"),
    )(proj, proj, proj, conv_w, sh)


def _tri_inverse_masks(c, upper):
    ri = lax.broadcasted_iota(jnp.int32, (c, c), 0)
    ci = lax.broadcasted_iota(jnp.int32, (c, c), 1)
    base_mask = (ri // GDN_INV_BASE) == (ci // GDN_INV_BASE)
    level_masks = []
    b = GDN_INV_BASE
    while b < c:
        mask = ((ri // (2 * b)) == (ci // (2 * b))) & ((ri // b) != (ci // b))
        level_masks.append(_take_blocks(mask, b, 0 if upper else 1))
        b *= 2
    return base_mask, level_masks


def _take_blocks(x, b, parity):
    return jnp.concatenate([x[s:s + b] for s in range(parity * b, x.shape[0], 2 * b)], axis=0)


def _merge_level(p, low, mask, b, upper):
    c = p.shape[0]
    par = 0 if upper else 1
    nblk = c // (2 * b)
    pb = p.astype(BF16)
    off = jnp.where(mask, _take_blocks(low, b, par), 0.0).astype(BF16)
    x = _dot(off, pb).astype(BF16)
    yield
    zero = jnp.zeros((b, c), BF16)
    pieces = []
    for i in range(nblk):
        xi = x[i * b:(i + 1) * b]
        pieces += [xi, zero] if upper else [zero, xi]
    y = _dot(_take_blocks(pb, b, par), jnp.concatenate(pieces, axis=0))
    yield
    rows = []
    for i in range(nblk):
        keep = p[(2 * i + 1 - par) * b:(2 * i + 2 - par) * b]
        upd = p[(2 * i + par) * b:(2 * i + par + 1) * b] - y[i * b:(i + 1) * b]
        rows += [upd, keep] if upper else [keep, upd]
    yield jnp.concatenate(rows, axis=0)


def _unit_tri_inverse(low, eye, masks, upper):
    c = low.shape[0]
    base_mask, level_masks = masks
    m0 = jnp.where(base_mask, -low, 0.0)
    p = eye + m0
    mb = m0.astype(BF16)
    m = _dot(mb, mb)
    yield
    steps = int(math.log2(GDN_INV_BASE))
    for s in range(2, steps + 1):
        mb = m.astype(BF16)
        if s < steps:
            r = _dot(mb, jnp.concatenate([mb, p.astype(BF16)], axis=1))
            m = r[:, :c]
            p = p + r[:, c:]
        else:
            p = p + _dot(mb, p.astype(BF16))
        yield
    b = GDN_INV_BASE
    for mask in level_masks:
        for out in _merge_level(p, low, mask, b, upper):
            if out is None:
                yield
        p = out
        b *= 2
    return p


def _delta_chain(qf, kf, kt, qk, kk, vf, s, beta, gcum, gcum_r, glast, eye, incl, strict, masks, upper):
    chunk = qf.shape[0]
    dec = jnp.where(incl, jnp.exp(gcum - gcum_r), 0.0)
    low = jnp.where(strict, kk * dec, 0.0) * beta
    eg = jnp.exp(gcum)
    ks_qs = _dot(jnp.concatenate([kf * beta * eg, qf * eg], axis=0).astype(BF16), s.astype(BF16))
    yield
    tinv = yield from _unit_tri_inverse(low, eye, masks, upper)
    v_new = _dot(tinv.astype(BF16), (vf * beta - ks_qs[:chunk]).astype(BF16)).astype(BF16)
    yield
    kend_t = kt * jnp.exp(glast - gcum_r)
    r = _dot(jnp.concatenate([(qk * dec).astype(BF16), kend_t.astype(BF16)], axis=0), v_new)
    yield
    yield ks_qs[chunk:] + r[:chunk], s * jnp.exp(glast) + r[chunk:]


def _run_interleaved(gens):
    results = [None] * len(gens)
    live = list(range(len(gens)))
    while live:
        nxt = []
        for i in live:
            try:
                results[i] = next(gens[i])
                nxt.append(i)
            except StopIteration:
                pass
        live = nxt
    return results


def _gdn_core_kernel(q_ref, k_ref, v_ref, gc_ref, gr_ref, o_ref, s_ref, *, group, chunk):
    t = q_ref.shape[1]
    nc = t // chunk
    ri = lax.broadcasted_iota(jnp.int32, (chunk, chunk), 0)
    ci = lax.broadcasted_iota(jnp.int32, (chunk, chunk), 1)
    eye = jnp.where(ri == ci, 1.0, 0.0).astype(F32)
    incl = (ri >= ci, ri <= ci)
    strict = (ri > ci, ri < ci)
    inv_masks = (_tri_inverse_masks(chunk, False), _tri_inverse_masks(chunk, True))
    s_ref[...] = jnp.zeros(s_ref.shape, F32)
    o_ref[...] = jnp.zeros(o_ref.shape, F32)

    def body(c, carry):
        chains = []
        rows = []
        for d in range(2):
            cc = c if d == 0 else nc - 1 - c
            rows.append(pl.ds(pl.multiple_of(cc * chunk, chunk), chunk))
            gcol = gc_ref[0, 0, rows[d], :]
            grow = gr_ref[0, 0, :, rows[d]]
            edge = chunk - 1 if d == 0 else 0
            for kh in range(group // 2):
                ksl = slice(kh * GDN_DK, (kh + 1) * GDN_DK)
                qb = q_ref[0, rows[d], ksl]
                kb = k_ref[0, rows[d], ksl]
                qf = qb.astype(F32)
                kf = kb.astype(F32)
                gram = _dot_nt(jnp.concatenate([qb, kb], axis=0), kb)
                kt = kf.T
                for j in (2 * kh, 2 * kh + 1):
                    cb = d * 2 * group + j
                    cg = cb + group
                    vf = v_ref[0, rows[d], j * GDN_DV:(j + 1) * GDN_DV].astype(F32)
                    chains.append(_delta_chain(
                        qf, kf, kt, gram[:chunk], gram[chunk:], vf, s_ref[d * group + j],
                        gcol[:, cb:cb + 1], gcol[:, cg:cg + 1], grow[cg:cg + 1, :], gcol[edge:edge + 1, cg:cg + 1],
                        eye, incl[d], strict[d], inv_masks[d], d == 1))
        outs = _run_interleaved(chains)
        for idx, (o_rows, s_new) in enumerate(outs):
            d, j = divmod(idx, group)
            o_ref[0, rows[d], j * GDN_DV:(j + 1) * GDN_DV] += o_rows
            s_ref[d * group + j] = s_new
        return carry

    lax.fori_loop(0, nc, body, 0)


def _gdn_core(qkv, gcol, grow, group, chunk):
    b, t, _ = qkv.shape
    ng = GDN_HV // group
    qw = (group // 2) * GDN_DK
    vw = group * GDN_DV
    kblk0 = GDN_QK_DIM // qw
    vblk0 = 2 * GDN_QK_DIM // vw
    once = pl.Buffered(1)
    return pl.pallas_call(
        functools.partial(_gdn_core_kernel, group=group, chunk=chunk),
        grid=(b, ng),
        in_specs=[
            pl.BlockSpec((1, t, qw), lambda bi, hg: (bi, 0, hg), pipeline_mode=once),
            pl.BlockSpec((1, t, qw), lambda bi, hg: (bi, 0, kblk0 + hg), pipeline_mode=once),
            pl.BlockSpec((1, t, vw), lambda bi, hg: (bi, 0, vblk0 + hg), pipeline_mode=once),
            pl.BlockSpec((1, 1, t, 4 * group), lambda bi, hg: (bi, hg, 0, 0)),
            pl.BlockSpec((1, 1, 4 * group, t), lambda bi, hg: (bi, hg, 0, 0)),
        ],
        out_specs=pl.BlockSpec((1, t, vw), lambda bi, hg: (bi, 0, hg), pipeline_mode=once),
        out_shape=jax.ShapeDtypeStruct((b, t, GDN_V_DIM), F32),
        scratch_shapes=[pltpu.VMEM((2 * group, GDN_DK, GDN_DV), F32)],
        compiler_params=_cparams("parallel", "parallel"),
    )(qkv, qkv, qkv, gcol, grow)


def _gated_deltanet_layer(x, p, j, li):
    b, t, d = x.shape
    n = b * t
    x2 = x.reshape(n, d)
    g_pre = p['pre_mix_g'][li]
    proj = _norm_matmul(x2, g_pre, p['gdn_w_main'][j], BF16, 1024, 2048)
    ba = _norm_matmul(x2, g_pre, p['gdn_w_gate'][j], F32, 1024, GDN_GATE_DIM)
    gates = _gdn_gates(ba, p['gdn_a_log'][j], p['gdn_dt_bias'][j], GDN_CHUNK, 512)
    grp = GDN_GROUP
    ng = GDN_HV // grp
    gcol = gates.reshape(b, t, 4, ng, grp).transpose(0, 3, 1, 2, 4).reshape(b, ng, t, 4 * grp)
    grow = jnp.swapaxes(gcol, 2, 3)
    qkv = _gdn_prep(proj.reshape(b, t, GDN_MAIN_DIM), p['gdn_conv_w'][j], 1024, 1024)
    o = _gdn_core(qkv, gcol, grow, grp, GDN_CHUNK)
    y = _matmul_norm_res(o.reshape(n, GDN_V_DIM), p['gdn_w_out'][j], x2, p['post_mix_g'][li], 256,
                         gate=(proj, GDN_CONV_DIM, p['gdn_norm_g'][j]))
    return y.reshape(b, t, d)


def _rel_bucket_table():
    qi = np.arange(SWA_BLOCK)[:, None]
    kj = np.arange(3 * SWA_BLOCK)[None, :]
    rel = kj - SWA_BLOCK - qi
    half = REL_BUCKETS // 2
    max_exact = half // 2
    nabs = np.abs(rel)
    large = max_exact + (np.log(np.maximum(nabs, 1).astype(np.float32) / max_exact)
                         / math.log(REL_MAX_DIST / max_exact) * (half - max_exact)).astype(np.int32)
    large = np.minimum(large, half - 1)
    bucket = np.where(rel > 0, half, 0) + np.where(nabs < max_exact, nabs, large)
    return np.where(nabs <= WINDOW, bucket, -1).astype(np.int32)


SWA_STEP_BLOCKS = 2


def _swa_kernel(q_ref, kvp_ref, kvc_ref, kvn_ref, bucket_ref, relb_ref, sink_ref, o_ref, bias_ref):
    b, i = pl.program_id(0), pl.program_id(1)
    nsteps = pl.num_programs(1)
    blk = SWA_BLOCK
    kvw = SWA_HKV * SWA_DH

    @pl.when((b == 0) & (i == 0))
    def _():
        bucket = bucket_ref[...]
        for hq in range(SWA_HQ):
            acc = jnp.full(bucket.shape, NEG_BIG, F32)
            for bk in range(REL_BUCKETS):
                acc = jnp.where(bucket == bk, relb_ref[bk, hq], acc)
            bias_ref[hq * blk:(hq + 1) * blk, :] = acc

    rows = SWA_GROUP * blk
    col = lax.broadcasted_iota(jnp.int32, (1, 3 * blk), 1)
    kv = jnp.concatenate([kvp_ref[0], kvc_ref[0], kvn_ref[0]], axis=0)

    def unit(qs, kcat, vcat, bias, sink):
        s = _dot_nt(qs, kcat) * SWA_DH ** -0.5 + bias
        yield
        m = jnp.maximum(jnp.max(s, axis=-1, keepdims=True), sink)
        yield
        pexp = jnp.exp(s - m)
        denom = jnp.sum(pexp, axis=-1, keepdims=True) + jnp.exp(sink - m)
        yield
        yield _dot(pexp.astype(BF16), vcat) / denom

    units = []
    for sb in range(SWA_STEP_BLOCKS):
        qrows = slice(sb * blk, (sb + 1) * blk)
        krows = slice(sb * blk, (sb + 3) * blk)
        outside = jnp.zeros(col.shape, jnp.bool_)
        if sb == 0:
            outside = outside | ((col < blk) & (i == 0))
        if sb == SWA_STEP_BLOCKS - 1:
            outside = outside | ((col >= 2 * blk) & (i == nsteps - 1))
        edge = jnp.where(outside, NEG_BIG, 0.0)
        for h in range(SWA_HKV):
            qs = jnp.concatenate(
                [q_ref[0, qrows, (h * SWA_GROUP + g) * SWA_DH:(h * SWA_GROUP + g + 1) * SWA_DH]
                 for g in range(SWA_GROUP)], axis=0)
            sink = jnp.concatenate(
                [jnp.full((blk, 1), sink_ref[h * SWA_GROUP + g], F32) for g in range(SWA_GROUP)], axis=0)
            units.append(unit(qs, kv[krows, h * SWA_DH:(h + 1) * SWA_DH],
                              kv[krows, kvw + h * SWA_DH:kvw + (h + 1) * SWA_DH],
                              bias_ref[h * rows:(h + 1) * rows, :] + edge, sink))
    for idx, o in enumerate(_run_interleaved(units)):
        sb, h = divmod(idx, SWA_HKV)
        for g in range(SWA_GROUP):
            osl = slice((h * SWA_GROUP + g) * SWA_DH, (h * SWA_GROUP + g + 1) * SWA_DH)
            o_ref[0, sb * blk:(sb + 1) * blk, osl] = o[g * blk:(g + 1) * blk].astype(o_ref.dtype)


def _swa_attention(qkv, rel_bias, sink):
    b, t, _ = qkv.shape
    nb = t // SWA_BLOCK
    sbk = SWA_STEP_BLOCKS
    qw = SWA_HQ * SWA_DH
    kvw2 = 2 * SWA_HKV * SWA_DH
    cblk = qw // kvw2
    return pl.pallas_call(
        _swa_kernel,
        grid=(b, nb // sbk),
        in_specs=[
            pl.BlockSpec((1, sbk * SWA_BLOCK, qw), lambda bi, i: (bi, i, 0)),
            pl.BlockSpec((1, SWA_BLOCK, kvw2), lambda bi, i: (bi, jnp.maximum(i * sbk - 1, 0), cblk)),
            pl.BlockSpec((1, sbk * SWA_BLOCK, kvw2), lambda bi, i: (bi, i, cblk)),
            pl.BlockSpec((1, SWA_BLOCK, kvw2), lambda bi, i: (bi, jnp.minimum((i + 1) * sbk, nb - 1), cblk)),
            pl.BlockSpec((SWA_BLOCK, 3 * SWA_BLOCK), lambda bi, i: (0, 0)),
            pl.BlockSpec(memory_space=pltpu.SMEM),
            pl.BlockSpec(memory_space=pltpu.SMEM),
        ],
        out_specs=pl.BlockSpec((1, sbk * SWA_BLOCK, qw), lambda bi, i: (bi, i, 0)),
        out_shape=jax.ShapeDtypeStruct((b, t, qw), BF16),
        scratch_shapes=[pltpu.VMEM((SWA_HQ * SWA_BLOCK, 3 * SWA_BLOCK), F32)],
        compiler_params=_cparams("arbitrary", "arbitrary"),
    )(qkv, qkv, qkv, qkv, jnp.asarray(_rel_bucket_table()), rel_bias.astype(F32), sink.astype(F32))


def _swa_layer(x, p, j, li):
    b, t, d = x.shape
    n = b * t
    x2 = x.reshape(n, d)
    qkv = _norm_matmul(x2, p['pre_mix_g'][li], p['swa_w_in'][j], BF16, 1024, 1024)
    o = _swa_attention(qkv.reshape(b, t, -1), p['rel_bias'], p['swa_sink'][j])
    y = _matmul_norm_res(o.reshape(n, d), p['swa_w_out'][j], x2, p['post_mix_g'][li], 512)
    return y.reshape(b, t, d)


def _fnet_chan_kernel(x_ref, g_ref, w_ref, o_ref, h_ref):
    gi = pl.program_id(1)

    @pl.when(gi == 0)
    def _():
        h_ref[...] = _rms(x_ref[...], g_ref[...]).astype(BF16)

    hg = h_ref[:, pl.ds(pl.multiple_of(gi * FNET_GW, FNET_GW), FNET_GW)]
    r = _dot(hg, w_ref[...])
    o_ref[0, 0] = r[:, :FNET_GW].astype(o_ref.dtype)
    o_ref[0, 1] = r[:, FNET_GW:].astype(o_ref.dtype)


def _fnet_chan(x, g, w_cs, tm):
    b, t, d = x.shape
    tm = min(tm, t)
    nt = t // tm
    return pl.pallas_call(
        _fnet_chan_kernel,
        grid=(b * nt, FNET_GROUPS),
        in_specs=[
            pl.BlockSpec((tm, d), lambda i, gi: (i, 0)),
            pl.BlockSpec((1, d), lambda i, gi: (0, 0)),
            pl.BlockSpec((FNET_GW, 2 * FNET_GW), lambda i, gi: (0, 0)),
        ],
        out_specs=pl.BlockSpec((1, 2, tm, FNET_GW), lambda i, gi: (i // nt, 0, i % nt, gi)),
        out_shape=jax.ShapeDtypeStruct((b, 2, t, d), BF16),
        scratch_shapes=[pltpu.VMEM((tm, d), BF16)],
        compiler_params=_cparams("parallel", "arbitrary"),
    )(x.reshape(b * t, d), g.reshape(1, d), w_cs)


def _bmm_kernel(a_ref, b_ref, o_ref, acc_ref):
    k = pl.program_id(2)

    @pl.when(k == 0)
    def _():
        acc_ref[...] = jnp.zeros(acc_ref.shape, F32)

    acc_ref[...] += _dot(a_ref[...], b_ref[0])

    @pl.when(k == pl.num_programs(2) - 1)
    def _():
        o_ref[0] = acc_ref[...].astype(o_ref.dtype)


def _shared_lhs_bmm(a, bm, tm, tk):
    m, kdim = a.shape
    b, _, d = bm.shape
    tm = min(tm, m)
    return pl.pallas_call(
        _bmm_kernel,
        grid=(b, m // tm, kdim // tk),
        in_specs=[
            pl.BlockSpec((tm, tk), lambda bi, i, k: (i, k)),
            pl.BlockSpec((1, tk, d), lambda bi, i, k: (bi, k, 0)),
        ],
        out_specs=pl.BlockSpec((1, tm, d), lambda bi, i, k: (bi, i, 0)),
        out_shape=jax.ShapeDtypeStruct((b, m, d), BF16),
        scratch_shapes=[pltpu.VMEM((tm, d), F32)],
        compiler_params=_cparams("parallel", "parallel", "arbitrary"),
    )(a, bm)


def _dft_tables(t):
    def angle(n):
        idx = jnp.arange(n, dtype=jnp.int32)
        return (idx[:, None] * idx[None, :] % n).astype(F32) * (2.0 * math.pi / n)

    ac = angle(FNET_GW)
    at = angle(t)
    scale = 1.0 / math.sqrt(t * FNET_GW)
    w_cs = jnp.concatenate([jnp.cos(ac), jnp.sin(ac)], axis=1).astype(BF16)
    f_t = (jnp.concatenate([jnp.cos(at), -jnp.sin(at)], axis=1) * scale).astype(BF16)
    return w_cs, f_t


def _fnet_layer(x, p, j, li):
    b, t, d = x.shape
    n = b * t
    w_cs, f_t = _dft_tables(t)
    pcs = _fnet_chan(x, p['pre_mix_g'][li], w_cs, 1024)
    y = _shared_lhs_bmm(f_t, pcs.reshape(b, 2 * t, d), 1024, 1024)
    out = _matmul_norm_res(y.reshape(n, d), p['fnet_w_out'][j], x.reshape(n, d), p['post_mix_g'][li], 512)
    return out.reshape(b, t, d)


def _trunk(x, p):
    b, t, d = x.shape
    for i in range(DEPTH):
        kind, j = i % N_MIXERS, i // N_MIXERS
        if kind == 0:
            x = _gated_deltanet_layer(x, p, j, i)
        elif kind == 1:
            x = _swa_layer(x, p, j, i)
        else:
            x = _fnet_layer(x, p, j, i)
        x = _ffn(x.reshape(b * t, d), p['pre_ffn_g'][i], p['ffn_w_gate_up'][i], p['ffn_w_down'][i],
                 p['post_ffn_g'][i], 512, FFN_TF).reshape(b, t, d)
    return x


def kernel(x_prompt, x_sample, pre_mix_g, post_mix_g, pre_ffn_g, post_ffn_g, gdn_w_in, gdn_conv_w, gdn_a_log,
           gdn_dt_bias, gdn_norm_g, gdn_w_out, swa_w_in, swa_sink, swa_w_out, rel_bias, fnet_w_out,
           ffn_w_gate_up, ffn_w_down):
    p = dict(
        pre_mix_g=pre_mix_g, post_mix_g=post_mix_g, pre_ffn_g=pre_ffn_g, post_ffn_g=post_ffn_g,
        gdn_w_main=gdn_w_in[:, :, :GDN_MAIN_DIM].astype(BF16),
        gdn_w_gate=gdn_w_in[:, :, GDN_MAIN_DIM:].astype(BF16),
        gdn_conv_w=gdn_conv_w, gdn_a_log=gdn_a_log, gdn_dt_bias=gdn_dt_bias, gdn_norm_g=gdn_norm_g,
        gdn_w_out=gdn_w_out.astype(BF16),
        swa_w_in=swa_w_in.astype(BF16), swa_sink=swa_sink, swa_w_out=swa_w_out.astype(BF16),
        rel_bias=rel_bias, fnet_w_out=fnet_w_out.astype(BF16),
        ffn_w_gate_up=_interleave_gate_up(ffn_w_gate_up.astype(BF16), FFN_TF), ffn_w_down=ffn_w_down.astype(BF16),
    )
    return _trunk(x_prompt, p), _trunk(x_sample, p)
```

```python
import functools
import math

import numpy as np
import jax
import jax.numpy as jnp
from jax import lax
from jax.experimental import pallas as pl
from jax.experimental.pallas import tpu as pltpu

D_MODEL = 2048
DEPTH = 4
N_MIXERS = 3

GDN_DK = 128
GDN_DV = 128
GDN_HK = D_MODEL // 128
GDN_HV = 2 * GDN_HK
GDN_QK_DIM = GDN_HK * GDN_DK
GDN_V_DIM = GDN_HV * GDN_DV
GDN_CONV_DIM = 2 * GDN_QK_DIM + GDN_V_DIM
GDN_CONV_W = 5
GDN_MAIN_DIM = GDN_CONV_DIM + GDN_V_DIM
GDN_GATE_DIM = 4 * GDN_HV
GDN_CHUNK = 128
GDN_GROUP = 8
GDN_INV_BASE = 16

SWA_DH = 128
SWA_HQ = D_MODEL // SWA_DH
SWA_HKV = 4
SWA_GROUP = SWA_HQ // SWA_HKV
WINDOW = 128
SWA_BLOCK = 128
REL_BUCKETS = 32
REL_MAX_DIST = 128

FNET_GROUPS = 4
FNET_GW = D_MODEL // FNET_GROUPS

FFN_HIDDEN = -(-8 * D_MODEL // (3 * 256)) * 256

RMS_EPS = 1e-6
NEG_BIG = -1e30

LANE = 128
BF16_SUBLANE = 16
VMEM_LIMIT_BYTES = 56 * 1024 * 1024

BF16 = jnp.bfloat16
F32 = jnp.float32


def _cparams(*sem):
    return pltpu.CompilerParams(dimension_semantics=sem, vmem_limit_bytes=VMEM_LIMIT_BYTES)


def _rms(x, g):
    return x * lax.rsqrt(jnp.mean(x * x, axis=-1, keepdims=True) + RMS_EPS) * g


def _silu(x):
    return x * (1.0 / (1.0 + jnp.exp(-x)))


def _dot(a, b):
    return jnp.dot(a, b, preferred_element_type=F32)


def _dot_nt(a, b):
    return lax.dot_general(a, b, (((1,), (1,)), ((), ())), preferred_element_type=F32)


def _norm_matmul_kernel(x_ref, g_ref, w_ref, o_ref, h_ref):
    @pl.when(pl.program_id(1) == 0)
    def _():
        h_ref[...] = _rms(x_ref[...], g_ref[...]).astype(h_ref.dtype)

    o_ref[...] = _dot(h_ref[...], w_ref[...]).astype(o_ref.dtype)


def _norm_matmul(x, g, w, out_dtype, tm, tn):
    n, d = x.shape
    m = w.shape[1]
    tm = min(tm, n)
    return pl.pallas_call(
        _norm_matmul_kernel,
        grid=(n // tm, m // tn),
        in_specs=[
            pl.BlockSpec((tm, d), lambda i, j: (i, 0)),
            pl.BlockSpec((1, d), lambda i, j: (0, 0)),
            pl.BlockSpec((d, tn), lambda i, j: (0, j)),
        ],
        out_specs=pl.BlockSpec((tm, tn), lambda i, j: (i, j)),
        out_shape=jax.ShapeDtypeStruct((n, m), out_dtype),
        scratch_shapes=[pltpu.VMEM((tm, d), BF16)],
        compiler_params=_cparams("parallel", "arbitrary"),
    )(x, g.reshape(1, d), w)


def _matmul_norm_res_kernel(*refs, gated):
    if gated:
        a_ref, z_ref, ng_ref, w_ref, x_ref, g_ref, o_ref = refs
        ng = ng_ref[...]
        parts = []
        for h in range(a_ref.shape[1] // GDN_DV):
            sl = slice(h * GDN_DV, (h + 1) * GDN_DV)
            parts.append((_rms(a_ref[:, sl], ng) * _silu(z_ref[:, sl].astype(F32))).astype(BF16))
        a = jnp.concatenate(parts, axis=1)
    else:
        a_ref, w_ref, x_ref, g_ref, o_ref = refs
        a = a_ref[...].astype(BF16)
    o_ref[...] = x_ref[...] + _rms(_dot(a, w_ref[...]), g_ref[...])


def _matmul_norm_res(a, w, x, g, tm, gate=None):
    n, kdim = a.shape
    d = w.shape[1]
    in_specs = [pl.BlockSpec((tm, kdim), lambda i: (i, 0))]
    args = [a]
    if gate is not None:
        zsrc, zcol0, norm_g = gate
        zblk = zcol0 // kdim
        in_specs += [pl.BlockSpec((tm, kdim), lambda i: (i, zblk)),
                     pl.BlockSpec((1, GDN_DV), lambda i: (0, 0))]
        args += [zsrc, norm_g.reshape(1, GDN_DV)]
    in_specs += [pl.BlockSpec((kdim, d), lambda i: (0, 0), pipeline_mode=pl.Buffered(1)),
                 pl.BlockSpec((tm, d), lambda i: (i, 0)),
                 pl.BlockSpec((1, d), lambda i: (0, 0))]
    args += [w, x, g.reshape(1, d)]
    return pl.pallas_call(
        functools.partial(_matmul_norm_res_kernel, gated=gate is not None),
        grid=(n // tm,),
        in_specs=in_specs,
        out_specs=pl.BlockSpec((tm, d), lambda i: (i, 0)),
        out_shape=jax.ShapeDtypeStruct((n, d), F32),
        compiler_params=_cparams("parallel"),
    )(*args)


def _ffn_kernel(x_ref, g1_ref, wg_ref, wu_ref, wd_ref, g2_ref, o_ref, h_ref, acc_ref):
    f = pl.program_id(1)

    @pl.when(f == 0)
    def _():
        h_ref[...] = _rms(x_ref[...], g1_ref[...]).astype(BF16)
        acc_ref[...] = jnp.zeros(acc_ref.shape, F32)

    h = h_ref[...]
    gate = _dot(h, wg_ref[...])
    up = _dot(h, wu_ref[...])
    acc_ref[...] += _dot((_silu(gate) * up).astype(BF16), wd_ref[...])

    @pl.when(f == pl.num_programs(1) - 1)
    def _():
        o_ref[...] = x_ref[...] + _rms(acc_ref[...], g2_ref[...])


def _ffn(x, g1, w_gu, w_d, g2, tm, tf):
    n, d = x.shape
    fdim = w_d.shape[0]
    nf = fdim // tf
    tm = min(tm, n)
    return pl.pallas_call(
        _ffn_kernel,
        grid=(n // tm, nf),
        in_specs=[
            pl.BlockSpec((tm, d), lambda i, f: (i, 0)),
            pl.BlockSpec((1, d), lambda i, f: (0, 0)),
            pl.BlockSpec((d, tf), lambda i, f: (0, f)),
            pl.BlockSpec((d, tf), lambda i, f: (0, nf + f)),
            pl.BlockSpec((tf, d), lambda i, f: (f, 0)),
            pl.BlockSpec((1, d), lambda i, f: (0, 0)),
        ],
        out_specs=pl.BlockSpec((tm, d), lambda i, f: (i, 0)),
        out_shape=jax.ShapeDtypeStruct((n, d), F32),
        scratch_shapes=[pltpu.VMEM((tm, d), BF16), pltpu.VMEM((tm, d), F32)],
        compiler_params=_cparams("parallel", "arbitrary"),
    )(x, g1.reshape(1, d), w_gu, w_gu, w_d, g2.reshape(1, d))


def _gdn_gate_kernel(ba_ref, nega_ref, dtb_ref, o_ref, *, chunk):
    rows = ba_ref.shape[0]
    lane = lax.broadcasted_iota(jnp.int32, (chunk, GDN_GATE_DIM), 1)
    is_dt = (lane % (2 * GDN_HV)) >= GDN_HV
    is_bwd = lane >= 2 * GDN_HV
    ri = lax.broadcasted_iota(jnp.int32, (chunk, chunk), 0)
    ci = lax.broadcasted_iota(jnp.int32, (chunk, chunk), 1)
    tril = jnp.where(ri >= ci, 1.0, 0.0).astype(BF16)
    triu = jnp.where(ri <= ci, 1.0, 0.0).astype(BF16)
    for c in range(rows // chunk):
        x = ba_ref[c * chunk:(c + 1) * chunk, :]
        beta = 1.0 / (1.0 + jnp.exp(-x))
        y = x + dtb_ref[...]
        sp = jnp.maximum(y, 0.0) + jnp.log1p(jnp.exp(-jnp.abs(y)))
        g = jnp.where(is_dt, nega_ref[...] * sp, 0.0)
        hi = g.astype(BF16)
        lo = (g - hi.astype(F32)).astype(BF16)
        pre = _dot(tril, hi) + _dot(tril, lo)
        suf = _dot(triu, hi) + _dot(triu, lo)
        o_ref[c * chunk:(c + 1) * chunk, :] = jnp.where(is_dt, jnp.where(is_bwd, suf, pre), beta)


def _gdn_gates(ba, a_log, dt_bias, chunk, tt):
    n = ba.shape[0]
    zeros = jnp.zeros((2, GDN_HV), F32)
    nega = jnp.stack([zeros, -jnp.exp(a_log.astype(F32))], axis=1).reshape(1, GDN_GATE_DIM)
    dtb = jnp.stack([zeros, dt_bias.astype(F32)], axis=1).reshape(1, GDN_GATE_DIM)
    return pl.pallas_call(
        functools.partial(_gdn_gate_kernel, chunk=chunk),
        grid=(n // tt,),
        in_specs=[pl.BlockSpec((tt, GDN_GATE_DIM), lambda i: (i, 0)),
                  pl.BlockSpec((1, GDN_GATE_DIM), lambda i: (0, 0)),
                  pl.BlockSpec((1, GDN_GATE_DIM), lambda i: (0, 0))],
        out_specs=pl.BlockSpec((tt, GDN_GATE_DIM), lambda i: (i, 0)),
        out_shape=jax.ShapeDtypeStruct((n, GDN_GATE_DIM), F32),
        compiler_params=_cparams("parallel"),
    )(ba, nega, dtb)


GDN_PREP_ROWS = 128
GDN_CONV_SIDE_TAPS = tuple(w for w in range(GDN_CONV_W) if w != GDN_CONV_W // 2)


def _conv_shift_matrix(halo):
    r = GDN_PREP_ROWS
    pad = GDN_CONV_W // 2
    sh = np.zeros((len(GDN_CONV_SIDE_TAPS) * r, r + 2 * halo), np.float32)
    for k, w in enumerate(GDN_CONV_SIDE_TAPS):
        sh[k * r + np.arange(r), halo + np.arange(r) + w - pad] = 1.0
    return sh


def _gdn_prep_kernel(cur_ref, prev_ref, next_ref, cw_ref, sh_ref, o_ref, ext_ref):
    i = pl.program_id(1)
    j = pl.program_id(2)
    tt, tc = cur_ref.shape[1], cur_ref.shape[2]
    halo = prev_ref.shape[1]
    pad = GDN_CONV_W // 2
    r = GDN_PREP_ROWS
    zero = jnp.zeros((halo, tc), BF16)
    ext_ref[0:halo, :] = jnp.where(i > 0, prev_ref[0], zero)
    ext_ref[halo:halo + tt, :] = cur_ref[0]
    ext_ref[halo + tt:2 * halo + tt, :] = jnp.where(i < pl.num_programs(1) - 1, next_ref[0], zero)
    col0 = j * tc

    def conv_silu(rb):
        window = ext_ref[rb * r:rb * r + r + 2 * halo, :]
        side = _dot(sh_ref[...], window)
        acc = window[halo:halo + r].astype(F32) * cw_ref[pad:pad + 1, :]
        for k, w in enumerate(GDN_CONV_SIDE_TAPS):
            acc = acc + side[k * r:(k + 1) * r] * cw_ref[w:w + 1, :]
        return _silu(acc)

    @pl.when(col0 < 2 * GDN_QK_DIM)
    def _():
        qscale = jnp.where(col0 < GDN_QK_DIM, GDN_DK ** -0.5, 1.0)
        for rb in range(tt // r):
            y = conv_silu(rb)
            for h in range(tc // GDN_DK):
                sl = slice(h * GDN_DK, (h + 1) * GDN_DK)
                yh = y[:, sl]
                inv = lax.rsqrt(jnp.sum(yh * yh, axis=-1, keepdims=True) + 1e-6)
                o_ref[0, rb * r:(rb + 1) * r, sl] = (yh * inv * qscale).astype(o_ref.dtype)

    @pl.when(col0 >= 2 * GDN_QK_DIM)
    def _():
        for rb in range(tt // r):
            o_ref[0, rb * r:(rb + 1) * r, :] = conv_silu(rb).astype(o_ref.dtype)


def _gdn_prep(proj, conv_w, tt, tc):
    b, t, _ = proj.shape
    halo = BF16_SUBLANE
    nh = t // halo
    r = tt // halo
    sh = jnp.asarray(_conv_shift_matrix(halo), BF16)
    return pl.pallas_call(
        _gdn_prep_kernel,
        grid=(b, t // tt, GDN_CONV_DIM // tc),
        in_specs=[
            pl.BlockSpec((1, tt, tc), lambda bi, i, j: (bi, i, j)),
            pl.BlockSpec((1, halo, tc), lambda bi, i, j: (bi, jnp.maximum(i * r - 1, 0), j)),
            pl.BlockSpec((1, halo, tc), lambda bi, i, j: (bi, jnp.minimum((i + 1) * r, nh - 1), j)),
            pl.BlockSpec((GDN_CONV_W, tc), lambda bi, i, j: (0, j)),
            pl.BlockSpec(sh.shape, lambda bi, i, j: (0, 0)),
        ],
        out_specs=pl.BlockSpec((1, tt, tc), lambda bi, i, j: (bi, i, j)),
        out_shape=jax.ShapeDtypeStruct((b, t, GDN_CONV_DIM), BF16),
        scratch_shapes=[pltpu.VMEM((tt + 2 * halo, tc), BF16)],
        compiler_params=_cparams("parallel", "parallel", "parallel"),
    )(proj, proj, proj, conv_w, sh)


def _tri_inverse_masks(c, upper):
    ri = lax.broadcasted_iota(jnp.int32, (c, c), 0)
    ci = lax.broadcasted_iota(jnp.int32, (c, c), 1)
    base_mask = (ri // GDN_INV_BASE) == (ci // GDN_INV_BASE)
    level_masks = []
    b = GDN_INV_BASE
    while b < c:
        mask = ((ri // (2 * b)) == (ci // (2 * b))) & ((ri // b) != (ci // b))
        level_masks.append(_take_blocks(mask, b, 0 if upper else 1))
        b *= 2
    return base_mask, level_masks


def _take_blocks(x, b, parity):
    return jnp.concatenate([x[s:s + b] for s in range(parity * b, x.shape[0], 2 * b)], axis=0)


def _merge_level(p, low, mask, b, upper):
    c = p.shape[0]
    par = 0 if upper else 1
    nblk = c // (2 * b)
    pb = p.astype(BF16)
    off = jnp.where(mask, _take_blocks(low, b, par), 0.0).astype(BF16)
    x = _dot(off, pb).astype(BF16)
    yield
    zero = jnp.zeros((b, c), BF16)
    pieces = []
    for i in range(nblk):
        xi = x[i * b:(i + 1) * b]
        pieces += [xi, zero] if upper else [zero, xi]
    y = _dot(_take_blocks(pb, b, par), jnp.concatenate(pieces, axis=0))
    yield
    rows = []
    for i in range(nblk):
        keep = p[(2 * i + 1 - par) * b:(2 * i + 2 - par) * b]
        upd = p[(2 * i + par) * b:(2 * i + par + 1) * b] - y[i * b:(i + 1) * b]
        rows += [upd, keep] if upper else [keep, upd]
    yield jnp.concatenate(rows, axis=0)


def _unit_tri_inverse(low, eye, masks, upper):
    c = low.shape[0]
    base_mask, level_masks = masks
    m0 = jnp.where(base_mask, -low, 0.0)
    p = eye + m0
    mb = m0.astype(BF16)
    m = _dot(mb, mb)
    yield
    steps = int(math.log2(GDN_INV_BASE))
    for s in range(2, steps + 1):
        mb = m.astype(BF16)
        if s < steps:
            r = _dot(mb, jnp.concatenate([mb, p.astype(BF16)], axis=1))
            m = r[:, :c]
            p = p + r[:, c:]
        else:
            p = p + _dot(mb, p.astype(BF16))
        yield
    b = GDN_INV_BASE
    for mask in level_masks:
        for out in _merge_level(p, low, mask, b, upper):
            if out is None:
                yield
        p = out
        b *= 2
    return p


def _delta_chain(qf, kf, kt, qk, kk, vf, s, beta, gcum, gcum_r, glast, eye, incl, strict, masks, upper):
    chunk = qf.shape[0]
    dec = jnp.where(incl, jnp.exp(gcum - gcum_r), 0.0)
    low = jnp.where(strict, kk * dec, 0.0) * beta
    tinv = yield from _unit_tri_inverse(low, eye, masks, upper)
    eg = jnp.exp(gcum)
    rhs = jnp.concatenate([vf * beta, kf * beta * eg], axis=1).astype(BF16)
    sol = _dot(tinv.astype(BF16), rhs)
    yield
    u, w = sol[:, :GDN_DV], sol[:, GDN_DV:]
    ws_qs = _dot(jnp.concatenate([w, qf * eg], axis=0).astype(BF16), s.astype(BF16))
    yield
    v_new = (u - ws_qs[:chunk]).astype(BF16)
    kend_t = kt * jnp.exp(glast - gcum_r)
    r = _dot(jnp.concatenate([(qk * dec).astype(BF16), kend_t.astype(BF16)], axis=0), v_new)
    yield
    yield ws_qs[chunk:] + r[:chunk], s * jnp.exp(glast) + r[chunk:]


def _run_interleaved(gens):
    results = [None] * len(gens)
    live = list(range(len(gens)))
    while live:
        nxt = []
        for i in live:
            try:
                results[i] = next(gens[i])
                nxt.append(i)
            except StopIteration:
                pass
        live = nxt
    return results


def _gdn_core_kernel(q_ref, k_ref, v_ref, gc_ref, gr_ref, o_ref, s_ref, *, group, chunk):
    t = q_ref.shape[1]
    nc = t // chunk
    ri = lax.broadcasted_iota(jnp.int32, (chunk, chunk), 0)
    ci = lax.broadcasted_iota(jnp.int32, (chunk, chunk), 1)
    eye = jnp.where(ri == ci, 1.0, 0.0).astype(F32)
    incl = (ri >= ci, ri <= ci)
    strict = (ri > ci, ri < ci)
    inv_masks = (_tri_inverse_masks(chunk, False), _tri_inverse_masks(chunk, True))
    s_ref[...] = jnp.zeros(s_ref.shape, F32)
    o_ref[...] = jnp.zeros(o_ref.shape, F32)

    def body(c, carry):
        chains = []
        rows = []
        for d in range(2):
            cc = c if d == 0 else nc - 1 - c
            rows.append(pl.ds(pl.multiple_of(cc * chunk, chunk), chunk))
            gcol = gc_ref[0, 0, rows[d], :]
            grow = gr_ref[0, 0, :, rows[d]]
            edge = chunk - 1 if d == 0 else 0
            for kh in range(group // 2):
                ksl = slice(kh * GDN_DK, (kh + 1) * GDN_DK)
                qb = q_ref[0, rows[d], ksl]
                kb = k_ref[0, rows[d], ksl]
                qf = qb.astype(F32)
                kf = kb.astype(F32)
                gram = _dot_nt(jnp.concatenate([qb, kb], axis=0), kb)
                kt = kf.T
                for j in (2 * kh, 2 * kh + 1):
                    cb = d * 2 * group + j
                    cg = cb + group
                    vf = v_ref[0, rows[d], j * GDN_DV:(j + 1) * GDN_DV].astype(F32)
                    chains.append(_delta_chain(
                        qf, kf, kt, gram[:chunk], gram[chunk:], vf, s_ref[d * group + j],
                        gcol[:, cb:cb + 1], gcol[:, cg:cg + 1], grow[cg:cg + 1, :], gcol[edge:edge + 1, cg:cg + 1],
                        eye, incl[d], strict[d], inv_masks[d], d == 1))
        outs = _run_interleaved(chains)
        for idx, (o_rows, s_new) in enumerate(outs):
            d, j = divmod(idx, group)
            o_ref[0, rows[d], j * GDN_DV:(j + 1) * GDN_DV] += o_rows
            s_ref[d * group + j] = s_new
        return carry

    lax.fori_loop(0, nc, body, 0)


def _gdn_core(qkv, gcol, grow, group, chunk):
    b, t, _ = qkv.shape
    ng = GDN_HV // group
    qw = (group // 2) * GDN_DK
    vw = group * GDN_DV
    kblk0 = GDN_QK_DIM // qw
    vblk0 = 2 * GDN_QK_DIM // vw
    once = pl.Buffered(1)
    return pl.pallas_call(
        functools.partial(_gdn_core_kernel, group=group, chunk=chunk),
        grid=(b, ng),
        in_specs=[
            pl.BlockSpec((1, t, qw), lambda bi, hg: (bi, 0, hg), pipeline_mode=once),
            pl.BlockSpec((1, t, qw), lambda bi, hg: (bi, 0, kblk0 + hg), pipeline_mode=once),
            pl.BlockSpec((1, t, vw), lambda bi, hg: (bi, 0, vblk0 + hg), pipeline_mode=once),
            pl.BlockSpec((1, 1, t, 4 * group), lambda bi, hg: (bi, hg, 0, 0)),
            pl.BlockSpec((1, 1, 4 * group, t), lambda bi, hg: (bi, hg, 0, 0)),
        ],
        out_specs=pl.BlockSpec((1, t, vw), lambda bi, hg: (bi, 0, hg), pipeline_mode=once),
        out_shape=jax.ShapeDtypeStruct((b, t, GDN_V_DIM), F32),
        scratch_shapes=[pltpu.VMEM((2 * group, GDN_DK, GDN_DV), F32)],
        compiler_params=_cparams("parallel", "parallel"),
    )(qkv, qkv, qkv, gcol, grow)


def _gated_deltanet_layer(x, p, j, li):
    b, t, d = x.shape
    n = b * t
    x2 = x.reshape(n, d)
    g_pre = p['pre_mix_g'][li]
    proj = _norm_matmul(x2, g_pre, p['gdn_w_main'][j], BF16, 1024, 2048)
    ba = _norm_matmul(x2, g_pre, p['gdn_w_gate'][j], F32, 1024, GDN_GATE_DIM)
    gates = _gdn_gates(ba, p['gdn_a_log'][j], p['gdn_dt_bias'][j], GDN_CHUNK, 512)
    grp = GDN_GROUP
    ng = GDN_HV // grp
    gcol = gates.reshape(b, t, 4, ng, grp).transpose(0, 3, 1, 2, 4).reshape(b, ng, t, 4 * grp)
    grow = jnp.swapaxes(gcol, 2, 3)
    qkv = _gdn_prep(proj.reshape(b, t, GDN_MAIN_DIM), p['gdn_conv_w'][j], 1024, 2048)
    o = _gdn_core(qkv, gcol, grow, grp, GDN_CHUNK)
    y = _matmul_norm_res(o.reshape(n, GDN_V_DIM), p['gdn_w_out'][j], x2, p['post_mix_g'][li], 256,
                         gate=(proj, GDN_CONV_DIM, p['gdn_norm_g'][j]))
    return y.reshape(b, t, d)


def _rel_bucket_table():
    qi = np.arange(SWA_BLOCK)[:, None]
    kj = np.arange(3 * SWA_BLOCK)[None, :]
    rel = kj - SWA_BLOCK - qi
    half = REL_BUCKETS // 2
    max_exact = half // 2
    nabs = np.abs(rel)
    large = max_exact + (np.log(np.maximum(nabs, 1).astype(np.float32) / max_exact)
                         / math.log(REL_MAX_DIST / max_exact) * (half - max_exact)).astype(np.int32)
    large = np.minimum(large, half - 1)
    bucket = np.where(rel > 0, half, 0) + np.where(nabs < max_exact, nabs, large)
    return np.where(nabs <= WINDOW, bucket, -1).astype(np.int32)


SWA_STEP_BLOCKS = 4


def _swa_kernel(q_ref, kvp_ref, kvc_ref, kvn_ref, bucket_ref, relb_ref, sink_ref, o_ref, bias_ref):
    b, i = pl.program_id(0), pl.program_id(1)
    nsteps = pl.num_programs(1)
    blk = SWA_BLOCK
    kvw = SWA_HKV * SWA_DH

    @pl.when((b == 0) & (i == 0))
    def _():
        bucket = bucket_ref[...]
        for hq in range(SWA_HQ):
            acc = jnp.full(bucket.shape, NEG_BIG, F32)
            for bk in range(REL_BUCKETS):
                acc = jnp.where(bucket == bk, relb_ref[bk, hq], acc)
            bias_ref[hq * blk:(hq + 1) * blk, :] = acc

    rows = SWA_GROUP * blk
    col = lax.broadcasted_iota(jnp.int32, (1, 3 * blk), 1)
    kv = jnp.concatenate([kvp_ref[0], kvc_ref[0], kvn_ref[0]], axis=0)

    def unit(qs, kcat, vcat, bias, sink):
        s = _dot_nt(qs, kcat) * SWA_DH ** -0.5 + bias
        yield
        m = jnp.maximum(jnp.max(s, axis=-1, keepdims=True), sink)
        yield
        pexp = jnp.exp(s - m)
        denom = jnp.sum(pexp, axis=-1, keepdims=True) + jnp.exp(sink - m)
        yield
        yield _dot(pexp.astype(BF16), vcat) / denom

    units = []
    for sb in range(SWA_STEP_BLOCKS):
        qrows = slice(sb * blk, (sb + 1) * blk)
        krows = slice(sb * blk, (sb + 3) * blk)
        outside = jnp.zeros(col.shape, jnp.bool_)
        if sb == 0:
            outside = outside | ((col < blk) & (i == 0))
        if sb == SWA_STEP_BLOCKS - 1:
            outside = outside | ((col >= 2 * blk) & (i == nsteps - 1))
        edge = jnp.where(outside, NEG_BIG, 0.0)
        for h in range(SWA_HKV):
            qs = jnp.concatenate(
                [q_ref[0, qrows, (h * SWA_GROUP + g) * SWA_DH:(h * SWA_GROUP + g + 1) * SWA_DH]
                 for g in range(SWA_GROUP)], axis=0)
            sink = jnp.concatenate(
                [jnp.full((blk, 1), sink_ref[h * SWA_GROUP + g], F32) for g in range(SWA_GROUP)], axis=0)
            units.append(unit(qs, kv[krows, h * SWA_DH:(h + 1) * SWA_DH],
                              kv[krows, kvw + h * SWA_DH:kvw + (h + 1) * SWA_DH],
                              bias_ref[h * rows:(h + 1) * rows, :] + edge, sink))
    for idx, o in enumerate(_run_interleaved(units)):
        sb, h = divmod(idx, SWA_HKV)
        for g in range(SWA_GROUP):
            osl = slice((h * SWA_GROUP + g) * SWA_DH, (h * SWA_GROUP + g + 1) * SWA_DH)
            o_ref[0, sb * blk:(sb + 1) * blk, osl] = o[g * blk:(g + 1) * blk].astype(o_ref.dtype)


def _swa_attention(qkv, rel_bias, sink):
    b, t, _ = qkv.shape
    nb = t // SWA_BLOCK
    sbk = SWA_STEP_BLOCKS
    qw = SWA_HQ * SWA_DH
    kvw2 = 2 * SWA_HKV * SWA_DH
    cblk = qw // kvw2
    return pl.pallas_call(
        _swa_kernel,
        grid=(b, nb // sbk),
        in_specs=[
            pl.BlockSpec((1, sbk * SWA_BLOCK, qw), lambda bi, i: (bi, i, 0)),
            pl.BlockSpec((1, SWA_BLOCK, kvw2), lambda bi, i: (bi, jnp.maximum(i * sbk - 1, 0), cblk)),
            pl.BlockSpec((1, sbk * SWA_BLOCK, kvw2), lambda bi, i: (bi, i, cblk)),
            pl.BlockSpec((1, SWA_BLOCK, kvw2), lambda bi, i: (bi, jnp.minimum((i + 1) * sbk, nb - 1), cblk)),
            pl.BlockSpec((SWA_BLOCK, 3 * SWA_BLOCK), lambda bi, i: (0, 0)),
            pl.BlockSpec(memory_space=pltpu.SMEM),
            pl.BlockSpec(memory_space=pltpu.SMEM),
        ],
        out_specs=pl.BlockSpec((1, sbk * SWA_BLOCK, qw), lambda bi, i: (bi, i, 0)),
        out_shape=jax.ShapeDtypeStruct((b, t, qw), BF16),
        scratch_shapes=[pltpu.VMEM((SWA_HQ * SWA_BLOCK, 3 * SWA_BLOCK), F32)],
        compiler_params=_cparams("arbitrary", "arbitrary"),
    )(qkv, qkv, qkv, qkv, jnp.asarray(_rel_bucket_table()), rel_bias.astype(F32), sink.astype(F32))


def _swa_layer(x, p, j, li):
    b, t, d = x.shape
    n = b * t
    x2 = x.reshape(n, d)
    qkv = _norm_matmul(x2, p['pre_mix_g'][li], p['swa_w_in'][j], BF16, 1024, 1024)
    o = _swa_attention(qkv.reshape(b, t, -1), p['rel_bias'], p['swa_sink'][j])
    y = _matmul_norm_res(o.reshape(n, d), p['swa_w_out'][j], x2, p['post_mix_g'][li], 512)
    return y.reshape(b, t, d)


def _fnet_chan_kernel(x_ref, g_ref, w_ref, o_ref, h_ref):
    gi = pl.program_id(1)

    @pl.when(gi == 0)
    def _():
        h_ref[...] = _rms(x_ref[...], g_ref[...]).astype(BF16)

    hg = h_ref[:, pl.ds(pl.multiple_of(gi * FNET_GW, FNET_GW), FNET_GW)]
    r = _dot(hg, w_ref[...])
    o_ref[0, 0] = r[:, :FNET_GW].astype(o_ref.dtype)
    o_ref[0, 1] = r[:, FNET_GW:].astype(o_ref.dtype)


def _fnet_chan(x, g, w_cs, tm):
    b, t, d = x.shape
    tm = min(tm, t)
    nt = t // tm
    return pl.pallas_call(
        _fnet_chan_kernel,
        grid=(b * nt, FNET_GROUPS),
        in_specs=[
            pl.BlockSpec((tm, d), lambda i, gi: (i, 0)),
            pl.BlockSpec((1, d), lambda i, gi: (0, 0)),
            pl.BlockSpec((FNET_GW, 2 * FNET_GW), lambda i, gi: (0, 0)),
        ],
        out_specs=pl.BlockSpec((1, 2, tm, FNET_GW), lambda i, gi: (i // nt, 0, i % nt, gi)),
        out_shape=jax.ShapeDtypeStruct((b, 2, t, d), BF16),
        scratch_shapes=[pltpu.VMEM((tm, d), BF16)],
        compiler_params=_cparams("parallel", "arbitrary"),
    )(x.reshape(b * t, d), g.reshape(1, d), w_cs)


def _bmm_kernel(a_ref, b_ref, o_ref, acc_ref):
    k = pl.program_id(2)

    @pl.when(k == 0)
    def _():
        acc_ref[...] = jnp.zeros(acc_ref.shape, F32)

    acc_ref[...] += _dot(a_ref[...], b_ref[0])

    @pl.when(k == pl.num_programs(2) - 1)
    def _():
        o_ref[0] = acc_ref[...].astype(o_ref.dtype)


def _shared_lhs_bmm(a, bm, tm, tk):
    m, kdim = a.shape
    b, _, d = bm.shape
    tm = min(tm, m)
    return pl.pallas_call(
        _bmm_kernel,
        grid=(b, m // tm, kdim // tk),
        in_specs=[
            pl.BlockSpec((tm, tk), lambda bi, i, k: (i, k)),
            pl.BlockSpec((1, tk, d), lambda bi, i, k: (bi, k, 0)),
        ],
        out_specs=pl.BlockSpec((1, tm, d), lambda bi, i, k: (bi, i, 0)),
        out_shape=jax.ShapeDtypeStruct((b, m, d), BF16),
        scratch_shapes=[pltpu.VMEM((tm, d), F32)],
        compiler_params=_cparams("parallel", "parallel", "arbitrary"),
    )(a, bm)


def _dft_tables(t):
    def angle(n):
        idx = jnp.arange(n, dtype=jnp.int32)
        return (idx[:, None] * idx[None, :] % n).astype(F32) * (2.0 * math.pi / n)

    ac = angle(FNET_GW)
    at = angle(t)
    scale = 1.0 / math.sqrt(t * FNET_GW)
    w_cs = jnp.concatenate([jnp.cos(ac), jnp.sin(ac)], axis=1).astype(BF16)
    f_t = (jnp.concatenate([jnp.cos(at), -jnp.sin(at)], axis=1) * scale).astype(BF16)
    return w_cs, f_t


def _fnet_layer(x, p, j, li):
    b, t, d = x.shape
    n = b * t
    w_cs, f_t = _dft_tables(t)
    pcs = _fnet_chan(x, p['pre_mix_g'][li], w_cs, 1024)
    y = _shared_lhs_bmm(f_t, pcs.reshape(b, 2 * t, d), 1024, 1024)
    out = _matmul_norm_res(y.reshape(n, d), p['fnet_w_out'][j], x.reshape(n, d), p['post_mix_g'][li], 512)
    return out.reshape(b, t, d)


def _trunk(x, p):
    b, t, d = x.shape
    for i in range(DEPTH):
        kind, j = i % N_MIXERS, i // N_MIXERS
        if kind == 0:
            x = _gated_deltanet_layer(x, p, j, i)
        elif kind == 1:
            x = _swa_layer(x, p, j, i)
        else:
            x = _fnet_layer(x, p, j, i)
        x = _ffn(x.reshape(b * t, d), p['pre_ffn_g'][i], p['ffn_w_gate_up'][i], p['ffn_w_down'][i],
                 p['post_ffn_g'][i], 512, 512).reshape(b, t, d)
    return x


def kernel(x_prompt, x_sample, pre_mix_g, post_mix_g, pre_ffn_g, post_ffn_g, gdn_w_in, gdn_conv_w, gdn_a_log,
           gdn_dt_bias, gdn_norm_g, gdn_w_out, swa_w_in, swa_sink, swa_w_out, rel_bias, fnet_w_out,
           ffn_w_gate_up, ffn_w_down):
    p = dict(
        pre_mix_g=pre_mix_g, post_mix_g=post_mix_g, pre_ffn_g=pre_ffn_g, post_ffn_g=post_ffn_g,
        gdn_w_main=gdn_w_in[:, :, :GDN_MAIN_DIM].astype(BF16),
        gdn_w_gate=gdn_w_in[:, :, GDN_MAIN_DIM:].astype(BF16),
        gdn_conv_w=gdn_conv_w, gdn_a_log=gdn_a_log, gdn_dt_bias=gdn_dt_bias, gdn_norm_g=gdn_norm_g,
        gdn_w_out=gdn_w_out.astype(BF16),
        swa_w_in=swa_w_in.astype(BF16), swa_sink=swa_sink, swa_w_out=swa_w_out.astype(BF16),
        rel_bias=rel_bias, fnet_w_out=fnet_w_out.astype(BF16),
        ffn_w_gate_up=ffn_w_gate_up.astype(BF16), ffn_w_down=ffn_w_down.astype(BF16),
    )
    return _trunk(x_prompt, p), _trunk(x_sample, p)
```
